```python
import jax, jax.numpy as jnp
from jax import lax
import numpy as np

D_MODEL = 2048
BATCH = 16
SEQ = 256
DEPTH = 1
DEC_BATCH = 2
DEC_SEQ = 2048
PAST_LEN = 256

GRID_W = 64
D_FF = 5632
CONV_W = 2048
CONV_K = 3
SSM_INNER = 2048
SSM_HEADDIM = 64
SSM_HEADS = SSM_INNER // SSM_HEADDIM
SSM_GROUPS = 4
SSM_STATE = 128
SSM_CONV_K = 3
CHUNK = 128
N_MOD = 9
EPS = 1e-6
XBC_W = SSM_INNER + 2 * SSM_GROUPS * SSM_STATE
IN_SPLITS = (CONV_W, CONV_W, CONV_W, SSM_INNER, XBC_W, SSM_HEADS, SSM_HEADS, D_MODEL, D_MODEL)
IN_COLS = 3 * CONV_W + SSM_INNER + XBC_W + 2 * SSM_HEADS + 2 * D_MODEL

kernel_name = "hybrid_conv_ssd_macaron_dit_step"


def rmsnorm(x, w):
    xf = x.astype(jnp.float32)
    y = xf * lax.rsqrt(jnp.mean(xf * xf, axis=-1, keepdims=True) + EPS)
    return (y * w.astype(jnp.float32)).astype(x.dtype)


def modulate(x, shift, scale):
    return x * (1 + scale) + shift


def swiglu(x, w_gate, w_up, w_down):
    return (jax.nn.silu(x @ w_gate) * (x @ w_up)) @ w_down


def dwconv_centred(u, w):
    k = w.shape[0]
    return lax.conv_general_dilated(u, w[:, None, :].astype(u.dtype), window_strides=(1,),
                                    padding=[(k // 2, k // 2)],
                                    dimension_numbers=('NWC', 'WIO', 'NWC'),
                                    feature_group_count=u.shape[-1])


def split_in(proj):
    idx, acc = [], 0
    for s in IN_SPLITS[:-1]:
        acc += s
        idx.append(acc)
    return jnp.split(proj, idx, axis=-1)


def segsum(a):
    t = a.shape[-1]
    ar = jnp.broadcast_to(a[..., :, None], a.shape + (t,))
    ar = jnp.where(jnp.tril(jnp.ones((t, t), bool), -1), ar, 0.0)
    s = jnp.cumsum(ar, axis=-2)
    return jnp.where(jnp.tril(jnp.ones((t, t), bool), 0), s, -jnp.inf)


def ssd_scan(x, dt, A, B, C, h0):
    f32 = jnp.float32
    b, l, h, p = x.shape
    g, n = B.shape[-2:]
    r = h // g
    c = l // CHUNK
    xd = (x.astype(f32) * dt[..., None]).reshape(b, c, CHUNK, g, r, p)
    a = jnp.transpose((dt * A).reshape(b, c, CHUNK, g, r), (0, 3, 4, 1, 2))
    a_cs = jnp.cumsum(a, axis=-1)
    Bc = B.astype(f32).reshape(b, c, CHUNK, g, n)
    Cc = C.astype(f32).reshape(b, c, CHUNK, g, n)
    Lmat = jnp.exp(segsum(a))
    CB = jnp.einsum('bclgn,bcsgn->bcgls', Cc, Bc)
    y_diag = jnp.einsum('bcgls,bgrcls,bcsgrp->bclgrp', CB, Lmat, xd)
    decay_states = jnp.exp(a_cs[..., -1:] - a_cs)
    states = jnp.einsum('bclgn,bgrcl,bclgrp->bcgrpn', Bc, decay_states, xd)
    states = jnp.concatenate([h0.astype(f32).reshape(b, 1, g, r, p, n), states], axis=1)
    chunk_tot = jnp.pad(a_cs[..., -1], ((0, 0), (0, 0), (0, 0), (1, 0)))
    decay_chunk = jnp.exp(segsum(chunk_tot))
    states = jnp.einsum('bgrzc,bcgrpn->bzgrpn', decay_chunk, states)
    prev, final = states[:, :-1], states[:, -1]
    y_off = jnp.einsum('bclgn,bcgrpn,bgrcl->bclgrp', Cc, prev, jnp.exp(a_cs))
    y = (y_diag + y_off).reshape(b, l, h, p)
    return y, final.reshape(b, h, p, n)


def mamba2_bidir(z, xbc, dtf_raw, dtb_raw, h0_f, h0_b, ssm_conv_w, ssm_conv_b, dt_bias_f, dt_bias_b,
                 a_log_f, a_log_b, d_skip, ssm_norm_w):
    b, l, _ = xbc.shape
    f32 = jnp.float32
    xbc = jax.nn.silu(dwconv_centred(xbc, ssm_conv_w) + ssm_conv_b)
    xs, Bs, Cs = jnp.split(xbc, [SSM_INNER, SSM_INNER + SSM_GROUPS * SSM_STATE], axis=-1)
    x = xs.reshape(b, l, SSM_HEADS, SSM_HEADDIM)
    B = Bs.reshape(b, l, SSM_GROUPS, SSM_STATE)
    C = Cs.reshape(b, l, SSM_GROUPS, SSM_STATE)
    dt_f = jax.nn.softplus(dtf_raw.astype(f32) + dt_bias_f.astype(f32))
    dt_b = jax.nn.softplus(dtb_raw.astype(f32) + dt_bias_b.astype(f32))
    A_f = -jnp.exp(a_log_f.astype(f32))
    A_b = -jnp.exp(a_log_b.astype(f32))
    y_f, hf = ssd_scan(x, dt_f, A_f, B, C, h0_f)
    flip = lambda t: jnp.flip(t, axis=1)
    y_b, hb = ssd_scan(flip(x), flip(dt_b), A_b, flip(B), flip(C), h0_b)
    y = y_f + flip(y_b) + x.astype(f32) * d_skip.astype(f32)[:, None]
    y = y.reshape(b, l, SSM_INNER).astype(z.dtype)
    y = rmsnorm(y * jax.nn.silu(z), ssm_norm_w)
    return y, hf, hb


def trunk_layer(h, cvec, row_len, h0_f, h0_b, w_ada, b_ada, norm1_w, ffn1_w_gate, ffn1_w_up, ffn1_w_down,
                norm2_w, w_in, conv_w, ssm_conv_w, ssm_conv_b, dt_bias_f, dt_bias_b, a_log_f, a_log_b,
                d_skip, ssm_norm_w, w_conv_out, w_ssm_out, w_o, norm3_w, ffn2_w_gate, ffn2_w_up, ffn2_w_down):
    b, L, _ = h.shape
    mod = (jax.nn.silu(cvec) @ w_ada + b_ada)[:, None, :]
    sh1, sc1, g1, sh2, sc2, g2, sh3, sc3, g3 = jnp.split(mod, N_MOD, axis=-1)
    u = modulate(rmsnorm(h, norm1_w), sh1, sc1)
    h = h + 0.5 * g1 * swiglu(u, ffn1_w_gate, ffn1_w_up, ffn1_w_down)
    u = modulate(rmsnorm(h, norm2_w), sh2, sc2)
    cb, cc, cx, z, xbc, dtf, dtb, gc, gs = split_in(u @ w_in)
    v = (cc * cx).reshape(b * (L // row_len), row_len, CONV_W)
    v = dwconv_centred(v, conv_w).reshape(b, L, CONV_W)
    y_conv = (cb * v) @ w_conv_out
    y_ssm, hf, hb = mamba2_bidir(z, xbc, dtf, dtb, h0_f, h0_b, ssm_conv_w, ssm_conv_b, dt_bias_f, dt_bias_b,
                                 a_log_f, a_log_b, d_skip, ssm_norm_w)
    y_ssm = y_ssm @ w_ssm_out
    merged = jax.nn.sigmoid(gc) * y_conv + jax.nn.sigmoid(gs) * y_ssm
    h = h + g2 * (merged @ w_o)
    u = modulate(rmsnorm(h, norm3_w), sh3, sc3)
    h = h + 0.5 * g3 * swiglu(u, ffn2_w_gate, ffn2_w_up, ffn2_w_down)
    return h, hf, hb


def setup_inputs(seed: int = 0) -> dict:
    key = jax.random.key(seed)
    ks = iter(jax.random.split(key, 40))
    nrm = lambda shape, s: jax.random.normal(next(ks), shape, jnp.float32) * s
    D = D_MODEL
    gain = lambda shape: 1.0 + nrm(shape, 0.02)
    dt0 = jnp.exp(jax.random.uniform(next(ks), (2, DEPTH, SSM_HEADS), jnp.float32,
                                     float(np.log(1e-3)), float(np.log(1e-1))))
    dt_bias = dt0 + jnp.log(-jnp.expm1(-dt0))
    a_log = jnp.log(jax.random.uniform(next(ks), (2, DEPTH, SSM_HEADS), jnp.float32, 1.0, 16.0))
    st_shape = (DEC_BATCH, DEPTH, SSM_HEADS, SSM_HEADDIM, SSM_STATE)
    return {
        "x_prompt": nrm((BATCH, SEQ, D), 1.0),
        "x_sample": nrm((DEC_BATCH, DEC_SEQ, D), 1.0),
        "c": nrm((DEC_BATCH, D), 1.0),
        "state_ssm_fwd": nrm(st_shape, 0.5),
        "state_ssm_bwd": nrm(st_shape, 0.5),
        "c_ctx": nrm((D,), 1.0),
        "w_ada": nrm((DEPTH, D, N_MOD * D), 0.5 * D ** -0.5),
        "b_ada": nrm((DEPTH, N_MOD * D), 0.02),
        "norm1_w": gain((DEPTH, D)),
        "ffn1_w_gate": nrm((DEPTH, D, D_FF), D ** -0.5),
        "ffn1_w_up": nrm((DEPTH, D, D_FF), D ** -0.5),
        "ffn1_w_down": nrm((DEPTH, D_FF, D), D_FF ** -0.5),
        "norm2_w": gain((DEPTH, D)),
        "w_in": nrm((DEPTH, D, IN_COLS), D ** -0.5),
        "conv_w": nrm((DEPTH, CONV_K, CONV_W), CONV_K ** -0.5),
        "ssm_conv_w": nrm((DEPTH, SSM_CONV_K, XBC_W), SSM_CONV_K ** -0.5),
        "ssm_conv_b": nrm((DEPTH, XBC_W), 0.02),
        "dt_bias_f": dt_bias[0],
        "dt_bias_b": dt_bias[1],
        "a_log_f": a_log[0],
        "a_log_b": a_log[1],
        "d_skip": gain((DEPTH, SSM_HEADS)),
        "ssm_norm_w": gain((DEPTH, SSM_INNER)),
        "w_conv_out": nrm((DEPTH, CONV_W, D), CONV_W ** -0.5),
        "w_ssm_out": nrm((DEPTH, SSM_INNER, D), SSM_INNER ** -0.5),
        "w_o": nrm((DEPTH, D, D), D ** -0.5),
        "norm3_w": gain((DEPTH, D)),
        "ffn2_w_gate": nrm((DEPTH, D, D_FF), D ** -0.5),
        "ffn2_w_up": nrm((DEPTH, D, D_FF), D ** -0.5),
        "ffn2_w_down": nrm((DEPTH, D_FF, D), D_FF ** -0.5),
        "final_norm_w": gain((D,)),
    }


def reference(x_prompt, x_sample, c, state_ssm_fwd, state_ssm_bwd, c_ctx, w_ada, b_ada, norm1_w,
              ffn1_w_gate, ffn1_w_up, ffn1_w_down, norm2_w, w_in, conv_w, ssm_conv_w, ssm_conv_b,
              dt_bias_f, dt_bias_b, a_log_f, a_log_b, d_skip, ssm_norm_w, w_conv_out, w_ssm_out, w_o,
              norm3_w, ffn2_w_gate, ffn2_w_up, ffn2_w_down, final_norm_w):
    n_ctx = x_prompt.shape[1]
    rows = x_sample.shape[1] // GRID_W
    zero_state = jnp.zeros((x_prompt.shape[0], SSM_HEADS, SSM_HEADDIM, SSM_STATE), x_prompt.dtype)
    hp, hs = x_prompt, x_sample
    new_f, new_b = [], []
    for l in range(DEPTH):
        params = (w_ada[l], b_ada[l], norm1_w[l], ffn1_w_gate[l], ffn1_w_up[l], ffn1_w_down[l],
                  norm2_w[l], w_in[l], conv_w[l], ssm_conv_w[l], ssm_conv_b[l], dt_bias_f[l], dt_bias_b[l],
                  a_log_f[l], a_log_b[l], d_skip[l], ssm_norm_w[l], w_conv_out[l], w_ssm_out[l], w_o[l],
                  norm3_w[l], ffn2_w_gate[l], ffn2_w_up[l], ffn2_w_down[l])
        hp, hf, hb = trunk_layer(hp, c_ctx[None, :], n_ctx, zero_state, zero_state, *params)
        new_f.append(hf.astype(x_prompt.dtype))
        new_b.append(hb.astype(x_prompt.dtype))
        hs, _, _ = trunk_layer(hs, c, rows and GRID_W, state_ssm_fwd[:, l], state_ssm_bwd[:, l], *params)
    y_prompt = rmsnorm(hp, final_norm_w)
    y_sample = rmsnorm(hs, final_norm_w)
    new_state_ssm_fwd = jnp.stack(new_f, axis=1)
    new_state_ssm_bwd = jnp.stack(new_b, axis=1)
    return (y_prompt, y_sample, new_state_ssm_fwd, new_state_ssm_bwd)
```

```python
import functools

import jax
import jax.numpy as jnp
from jax import lax
from jax.experimental import pallas as pl
from jax.experimental.pallas import tpu as pltpu

D_MODEL = 2048
D_FF = 5632
GRID_W = 64
CONV_W = 2048
SSM_INNER = 2048
SSM_HEADDIM = 64
SSM_HEADS = SSM_INNER // SSM_HEADDIM
SSM_GROUPS = 4
SSM_STATE = 128
CHUNK = 128
N_MOD = 9
EPS = 1e-6
XBC_W = SSM_INNER + 2 * SSM_GROUPS * SSM_STATE

HEADS_PER_GROUP = SSM_HEADS // SSM_GROUPS
PAIRS_PER_GROUP = HEADS_PER_GROUP // 2
GROUP_W = HEADS_PER_GROUP * SSM_HEADDIM
LANES = 128
NEG_BIG = -1e30

TM = 512
TN = 512
TF = 512
TN_ADA = 1024
VMEM_LIMIT = 52 * 1024 * 1024

F32 = jnp.float32
BF16 = jnp.bfloat16


def _dot(a, b):
    return jnp.dot(a, b, preferred_element_type=F32)


def _dot_nt(a, b):
    return lax.dot_general(a, b, (((1,), (1,)), ((), ())), preferred_element_type=F32)


def _silu(x):
    return x * jax.nn.sigmoid(x)


def _rms(x, w):
    ms = jnp.mean(x * x, axis=-1, keepdims=True)
    return x * lax.rsqrt(ms + EPS) * w


def _params(sem):
    return pltpu.CompilerParams(dimension_semantics=sem, vmem_limit_bytes=VMEM_LIMIT)


def _mod_kernel(c_ref, w_ref, b_ref, o_ref):
    a = _silu(c_ref[...]).astype(BF16)
    o_ref[...] = _dot(a, w_ref[...].astype(BF16)) + b_ref[...]


def _mod_call(cvec, w_ada, b_ada):
    n = w_ada.shape[1]
    return pl.pallas_call(
        _mod_kernel,
        grid=(n // TN_ADA,),
        in_specs=[
            pl.BlockSpec((8, D_MODEL), lambda j: (0, 0)),
            pl.BlockSpec((D_MODEL, TN_ADA), lambda j: (0, j)),
            pl.BlockSpec((1, TN_ADA), lambda j: (0, j)),
        ],
        out_specs=pl.BlockSpec((8, TN_ADA), lambda j: (0, j)),
        out_shape=jax.ShapeDtypeStruct((8, n), F32),
        compiler_params=_params(("arbitrary",)),
        name="adaln_mod",
    )(cvec, w_ada, b_ada.reshape(1, n))


def _ffn_kernel(h_ref, mod_ref, nw_ref, wg_ref, wu_ref, wd_ref, nw2_ref, *rest, mod_base, final):
    if final:
        o_ref, u_scr, acc_scr = rest
    else:
        o_ref, u_ref, u_scr, acc_scr = rest
    j = pl.program_id(1)
    nj = pl.num_programs(1)

    @pl.when(j == 0)
    def _():
        shift = mod_ref[0, mod_base:mod_base + 1, :]
        scale = mod_ref[0, mod_base + 1:mod_base + 2, :]
        u = _rms(h_ref[...], nw_ref[...]) * (1.0 + scale) + shift
        u_scr[...] = u.astype(BF16)
        acc_scr[...] = jnp.zeros_like(acc_scr)

    u = u_scr[...]
    a = _silu(_dot(u, wg_ref[...])) * _dot(u, wu_ref[...])
    acc_scr[...] += _dot(a.astype(BF16), wd_ref[...])

    @pl.when(j == nj - 1)
    def _():
        gate = mod_ref[0, mod_base + 2:mod_base + 3, :]
        hn = h_ref[...] + 0.5 * gate * acc_scr[...]
        if final:
            o_ref[...] = _rms(hn, nw2_ref[...])
        else:
            o_ref[...] = hn
            shift = mod_ref[0, mod_base + 3:mod_base + 4, :]
            scale = mod_ref[0, mod_base + 4:mod_base + 5, :]
            u_ref[...] = (_rms(hn, nw2_ref[...]) * (1.0 + scale) + shift).astype(BF16)


def _ffn_call(h, mod, nw, wg, wu, wd, nw2, *, mod_base, final):
    t = h.shape[0]
    tpm = t // mod.shape[0] // TM
    row = pl.BlockSpec((TM, D_MODEL), lambda i, j: (i, 0))
    vec = pl.BlockSpec((1, D_MODEL), lambda i, j: (0, 0))
    out_shape = [jax.ShapeDtypeStruct((t, D_MODEL), F32)]
    out_specs = [row]
    if not final:
        out_shape.append(jax.ShapeDtypeStruct((t, D_MODEL), BF16))
        out_specs.append(row)
    return pl.pallas_call(
        functools.partial(_ffn_kernel, mod_base=mod_base, final=final),
        grid=(t // TM, D_FF // TF),
        in_specs=[
            row,
            pl.BlockSpec((1, N_MOD, D_MODEL), lambda i, j: (i // tpm, 0, 0)),
            vec,
            pl.BlockSpec((D_MODEL, TF), lambda i, j: (0, j)),
            pl.BlockSpec((D_MODEL, TF), lambda i, j: (0, j)),
            pl.BlockSpec((TF, D_MODEL), lambda i, j: (j, 0)),
            vec,
        ],
        out_specs=out_specs,
        out_shape=out_shape,
        scratch_shapes=[pltpu.VMEM((TM, D_MODEL), BF16), pltpu.VMEM((TM, D_MODEL), F32)],
        compiler_params=_params(("parallel", "arbitrary")),
        name="ffn_final" if final else "ffn",
    )(h, mod, nw, wg, wu, wd, nw2)


def _convproj_kernel(u_ref, wb_ref, wc_ref, wx_ref, cw_ref, o_ref, *, row_len):
    u = u_ref[...]
    v = _dot(u, wc_ref[...]) * _dot(u, wx_ref[...])
    pos = lax.broadcasted_iota(jnp.int32, v.shape, 0) & (row_len - 1)
    v_prev = jnp.where(pos == 0, 0.0, pltpu.roll(v, 1, 0))
    v_next = jnp.where(pos == row_len - 1, 0.0, pltpu.roll(v, TM - 1, 0))
    cw = cw_ref[...]
    conv = cw[0:1, :] * v_prev + cw[1:2, :] * v + cw[2:3, :] * v_next
    o_ref[...] = (_dot(u, wb_ref[...]) * conv).astype(BF16)


def _convproj_call(u, wb, wc, wx, conv_w, *, row_len):
    t = u.shape[0]
    wspec = pl.BlockSpec((D_MODEL, TN), lambda i, j: (0, j))
    return pl.pallas_call(
        functools.partial(_convproj_kernel, row_len=row_len),
        grid=(t // TM, CONV_W // TN),
        in_specs=[
            pl.BlockSpec((TM, D_MODEL), lambda i, j: (i, 0)),
            wspec, wspec, wspec,
            pl.BlockSpec((3, TN), lambda i, j: (0, j)),
        ],
        out_specs=pl.BlockSpec((TM, TN), lambda i, j: (i, j)),
        out_shape=jax.ShapeDtypeStruct((t, CONV_W), BF16),
        compiler_params=_params(("parallel", "arbitrary")),
        name="conv_proj",
    )(u, wb, wc, wx, conv_w)


def _softplus(x):
    return jnp.maximum(x, 0.0) + jnp.log1p(jnp.exp(-jnp.abs(x)))


def _xbcproj_kernel(u_ref, w_ref, wdt_ref, dtb_ref, o_ref, dt_ref):
    u = u_ref[...]
    o_ref[...] = _dot(u, w_ref[...])

    @pl.when(pl.program_id(1) == 0)
    def _():
        dt_ref[...] = _softplus(_dot(u, wdt_ref[...]) + dtb_ref[...])


def _xbcproj_call(u, w_xbc, w_dt, dt_bias):
    t = u.shape[0]
    ndt = w_dt.shape[1]
    return pl.pallas_call(
        _xbcproj_kernel,
        grid=(t // TM, XBC_W // TN),
        in_specs=[
            pl.BlockSpec((TM, D_MODEL), lambda i, j: (i, 0)),
            pl.BlockSpec((D_MODEL, TN), lambda i, j: (0, j)),
            pl.BlockSpec((D_MODEL, ndt), lambda i, j: (0, 0)),
            pl.BlockSpec((1, ndt), lambda i, j: (0, 0)),
        ],
        out_specs=[
            pl.BlockSpec((TM, TN), lambda i, j: (i, j)),
            pl.BlockSpec((TM, ndt), lambda i, j: (i, 0)),
        ],
        out_shape=[
            jax.ShapeDtypeStruct((t, XBC_W), F32),
            jax.ShapeDtypeStruct((t, ndt), F32),
        ],
        compiler_params=_params(("parallel", "arbitrary")),
        name="xbc_proj",
    )(u, w_xbc, w_dt, dt_bias)


def _split3(a):
    a1 = a.astype(BF16)
    r1 = a - a1.astype(F32)
    a2 = r1.astype(BF16)
    a3 = (r1 - a2.astype(F32)).astype(BF16)
    return a1, a2, a3


def _ssd_kernel(x_ref, b_ref, c_ref, cwx_ref, cwb_ref, cwc_ref, cbx_ref, cbb_ref, cbc_ref,
                dt_ref, alog_ref, dsk_ref, *rest, seq_len, has_h0):
    if has_h0:
        h0f_ref, h0b_ref, y_ref, xc, bc, cc, cum, cum_t, dt_t, sb_in, sf, sb = rest
    else:
        y_ref, hf_ref, hb_ref, xc, bc, cc, cum, cum_t, dt_t, sb_in, sf, sb = rest
    q = CHUNK
    nc = seq_len // q
    hpg = HEADS_PER_GROUP

    lane = lax.broadcasted_iota(jnp.int32, (q, LANES), 1)
    row = lax.broadcasted_iota(jnp.int32, (q, q), 0)
    col = lax.broadcasted_iota(jnp.int32, (q, q), 1)
    tril = row >= col
    triu = row <= col
    tril_b = jnp.where(tril, 1.0, 0.0).astype(BF16)
    triu_b = jnp.where(triu, 1.0, 0.0).astype(BF16)
    a_row = jnp.where(lane[0:1, :] < 2 * hpg, -jnp.exp(alog_ref[...]), 0.0)

    def rows(c):
        return pl.ds(pl.multiple_of(c * q, q), q)

    def conv_chunk(ref, w_ref, bias_ref, c):
        r0 = c * q
        cur = ref[rows(c), :]
        prev8 = ref[pl.ds(pl.multiple_of(jnp.maximum(r0 - 8, 0), 8), 8), :]
        next8 = ref[pl.ds(pl.multiple_of(jnp.minimum(r0 + q, seq_len - 8), 8), 8), :]
        prev_row = jnp.where(c > 0, prev8[7:8, :], 0.0)
        next_row = jnp.where(c < nc - 1, next8[0:1, :], 0.0)
        ridx = lax.broadcasted_iota(jnp.int32, cur.shape, 0)
        x_prev = jnp.where(ridx == 0, prev_row, pltpu.roll(cur, 1, 0))
        x_next = jnp.where(ridx == q - 1, next_row, pltpu.roll(cur, q - 1, 0))
        w = w_ref[...]
        return _silu(w[0:1, :] * x_prev + w[1:2, :] * cur + w[2:3, :] * x_next + bias_ref[...])

    def prep(c, carry):
        xc[rows(c), :] = conv_chunk(x_ref, cwx_ref, cbx_ref, c)
        bc[rows(c), :] = conv_chunk(b_ref, cwb_ref, cbb_ref, c)
        cc[rows(c), :] = conv_chunk(c_ref, cwc_ref, cbc_ref, c)
        dt = dt_ref[rows(c), :]
        pre = jnp.zeros((q, LANES), F32)
        suf = jnp.zeros((q, LANES), F32)
        for piece in _split3(dt * a_row):
            pre += _dot(tril_b, piece)
            suf += _dot(triu_b, piece)
        cm = jnp.where(lane < hpg, pre, suf)
        cum[rows(c), :] = cm
        cum_t[c] = cm.T
        dt_t[c] = dt.T
        return carry

    lax.fori_loop(0, nc, prep, 0)

    if has_h0:
        sf[...] = h0f_ref[0]
        sb[...] = h0b_ref[0]
    else:
        sf[...] = jnp.zeros_like(sf)
        sb[...] = jnp.zeros_like(sb)

    def pair_weights(cm, dt, edge, base, p):
        l0 = base + 2 * p
        w0 = jnp.exp(edge[:, l0:l0 + 1] - cm[:, l0:l0 + 1]) * dt[:, l0:l0 + 1]
        w1 = jnp.exp(edge[:, l0 + 1:l0 + 2] - cm[:, l0 + 1:l0 + 2]) * dt[:, l0 + 1:l0 + 2]
        wsel = jnp.where(lane < SSM_HEADDIM, w0, w1)
        dec = jnp.where(row < SSM_HEADDIM, jnp.exp(edge[:, l0:l0 + 1]), jnp.exp(edge[:, l0 + 1:l0 + 2]))
        return wsel, dec

    def state_update(state_ref, p, x_pair, wsel, dec, b_bf):
        xw_t = (x_pair * wsel).T.astype(BF16)
        state_ref[p] = dec * state_ref[p] + _dot(xw_t, b_bf)

    def bwd_states(k, carry):
        c = nc - 1 - k
        sb_in[c] = sb[...]
        cm = cum[rows(c), :]
        dt = dt_ref[rows(c), :]
        b_bf = bc[rows(c), :].astype(BF16)
        first = cm[0:1, :]
        for p in range(PAIRS_PER_GROUP):
            x_pair = xc[rows(c), p * LANES:(p + 1) * LANES]
            wsel, dec = pair_weights(cm, dt, first, hpg, p)
            state_update(sb, p, x_pair, wsel, dec, b_bf)
        return carry

    lax.fori_loop(0, nc, bwd_states, 0)
    if not has_h0:
        hb_ref[0] = sb[...]

    def fwd(c, carry):
        cm = cum[rows(c), :]
        cm_t = cum_t[c]
        dtt = dt_t[c]
        dt = dt_ref[rows(c), :]
        b_bf = bc[rows(c), :].astype(BF16)
        c_bf = cc[rows(c), :].astype(BF16)
        g = _dot_nt(c_bf, b_bf)
        last = cm[q - 1:q, :]
        for p in range(PAIRS_PER_GROUP):
            x_pair = xc[rows(c), p * LANES:(p + 1) * LANES]
            y = dsk_ref[:, p * LANES:(p + 1) * LANES] * x_pair
            for e in range(2):
                lf = 2 * p + e
                lb = hpg + lf
                m_f = jnp.exp(jnp.where(tril, cm[:, lf:lf + 1] - cm_t[lf:lf + 1, :], NEG_BIG)) * dtt[lf:lf + 1, :]
                m_b = jnp.exp(jnp.where(triu, cm[:, lb:lb + 1] - cm_t[lb:lb + 1, :], NEG_BIG)) * dtt[lb:lb + 1, :]
                m = ((m_f + m_b) * g).astype(BF16)
                in_head = (lane < SSM_HEADDIM) if e == 0 else (lane >= SSM_HEADDIM)
                y += _dot(m, jnp.where(in_head, x_pair, 0.0).astype(BF16))
            lf = 2 * p
            lb = hpg + lf
            e_f = jnp.where(lane < SSM_HEADDIM, jnp.exp(cm[:, lf:lf + 1]), jnp.exp(cm[:, lf + 1:lf + 2]))
            e_b = jnp.where(lane < SSM_HEADDIM, jnp.exp(cm[:, lb:lb + 1]), jnp.exp(cm[:, lb + 1:lb + 2]))
            y += e_f * _dot_nt(c_bf, sf[p].astype(BF16))
            y += e_b * _dot_nt(c_bf, sb_in[c, p].astype(BF16))
            y_ref[rows(c), p * LANES:(p + 1) * LANES] = y
            wsel, dec = pair_weights(cm, dt, last, 0, p)
            state_update(sf, p, x_pair, wsel, dec, b_bf)
        return carry

    lax.fori_loop(0, nc, fwd, 0)
    if not has_h0:
        hf_ref[0] = sf[...]


def _ssd_call(xbc, dt, conv_w, conv_b, alog, dskip, h0f, h0b, *, seq_len):
    t = xbc.shape[0]
    nseq = t // seq_len
    nc = seq_len // CHUNK
    has_h0 = h0f is not None
    gx = SSM_INNER // LANES
    gc = gx + SSM_GROUPS
    st_spec = pl.BlockSpec((1, PAIRS_PER_GROUP, LANES, SSM_STATE), lambda s, g: (s, g, 0, 0))
    in_specs = [
        pl.BlockSpec((seq_len, GROUP_W), lambda s, g: (s, g)),
        pl.BlockSpec((seq_len, LANES), lambda s, g: (s, gx + g)),
        pl.BlockSpec((seq_len, LANES), lambda s, g: (s, gc + g)),
        pl.BlockSpec((3, GROUP_W), lambda s, g: (0, g)),
        pl.BlockSpec((3, LANES), lambda s, g: (0, gx + g)),
        pl.BlockSpec((3, LANES), lambda s, g: (0, gc + g)),
        pl.BlockSpec((1, GROUP_W), lambda s, g: (0, g)),
        pl.BlockSpec((1, LANES), lambda s, g: (0, gx + g)),
        pl.BlockSpec((1, LANES), lambda s, g: (0, gc + g)),
        pl.BlockSpec((seq_len, LANES), lambda s, g: (s, g)),
        pl.BlockSpec((1, LANES), lambda s, g: (0, g)),
        pl.BlockSpec((1, GROUP_W), lambda s, g: (0, g)),
    ]
    args = [xbc, xbc, xbc, conv_w, conv_w, conv_w, conv_b, conv_b, conv_b, dt, alog, dskip]
    y_shape = jax.ShapeDtypeStruct((t, SSM_INNER), F32)
    y_spec = pl.BlockSpec((seq_len, GROUP_W), lambda s, g: (s, g))
    st_shape = jax.ShapeDtypeStruct((nseq, SSM_HEADS // 2, LANES, SSM_STATE), F32)
    if has_h0:
        in_specs += [st_spec, st_spec]
        args += [h0f, h0b]
        out_specs, out_shape = [y_spec], [y_shape]
    else:
        out_specs, out_shape = [y_spec, st_spec, st_spec], [y_shape, st_shape, st_shape]
    pair_state = (PAIRS_PER_GROUP, LANES, SSM_STATE)
    return pl.pallas_call(
        functools.partial(_ssd_kernel, seq_len=seq_len, has_h0=has_h0),
        grid=(nseq, SSM_GROUPS),
        in_specs=in_specs,
        out_specs=out_specs,
        out_shape=out_shape,
        scratch_shapes=[
            pltpu.VMEM((seq_len, GROUP_W), F32),
            pltpu.VMEM((seq_len, LANES), F32),
            pltpu.VMEM((seq_len, LANES), F32),
            pltpu.VMEM((seq_len, LANES), F32),
            pltpu.VMEM((nc, LANES, CHUNK), F32),
            pltpu.VMEM((nc, LANES, CHUNK), F32),
            pltpu.VMEM((nc,) + pair_state, F32),
            pltpu.VMEM(pair_state, F32),
            pltpu.VMEM(pair_state, F32),
        ],
        compiler_params=_params(("parallel", "arbitrary")),
        name="ssd_latent" if has_h0 else "ssd_context",
    )(*args)


def _gatednorm_kernel(u_ref, y_ref, wz_ref, nw_ref, o_ref, t_scr, ss_scr):
    j = pl.program_id(1)
    nj = SSM_INNER // TN
    t = y_ref[...] * _silu(_dot(u_ref[...], wz_ref[...]))
    t_scr[j] = t
    part = jnp.sum(t * t, axis=-1, keepdims=True)

    @pl.when(j == 0)
    def _():
        ss_scr[...] = part

    @pl.when(j > 0)
    def _():
        ss_scr[...] += part

    @pl.when(j == nj - 1)
    def _():
        r = lax.rsqrt(ss_scr[...] * (1.0 / SSM_INNER) + EPS)
        for k in range(nj):
            sl = slice(k * TN, (k + 1) * TN)
            o_ref[:, sl] = (t_scr[k] * r * nw_ref[:, sl]).astype(BF16)


def _gatednorm_call(u, y, wz, nw):
    t = u.shape[0]
    nj = SSM_INNER // TN
    return pl.pallas_call(
        _gatednorm_kernel,
        grid=(t // TM, nj),
        in_specs=[
            pl.BlockSpec((TM, D_MODEL), lambda i, j: (i, 0)),
            pl.BlockSpec((TM, TN), lambda i, j: (i, j)),
            pl.BlockSpec((D_MODEL, TN), lambda i, j: (0, j)),
            pl.BlockSpec((1, SSM_INNER), lambda i, j: (0, 0)),
        ],
        out_specs=pl.BlockSpec((TM, SSM_INNER), lambda i, j: (i, 0)),
        out_shape=jax.ShapeDtypeStruct((t, SSM_INNER), BF16),
        scratch_shapes=[pltpu.VMEM((nj, TM, TN), F32), pltpu.VMEM((TM, 1), F32)],
        compiler_params=_params(("parallel", "arbitrary")),
        name="gated_norm",
    )(u, y, wz, nw)


def _merge_kernel(u_ref, cv_ref, yn_ref, wgc_ref, wco_ref, wgs_ref, wso_ref, o_ref):
    u = u_ref[...]
    m = jax.nn.sigmoid(_dot(u, wgc_ref[...])) * _dot(cv_ref[...], wco_ref[...])
    m += jax.nn.sigmoid(_dot(u, wgs_ref[...])) * _dot(yn_ref[...], wso_ref[...])
    o_ref[...] = m.astype(BF16)


def _merge_call(u, cv, yn, wgc, wco, wgs, wso):
    t = u.shape[0]
    row = pl.BlockSpec((TM, D_MODEL), lambda i, j: (i, 0))
    wspec = pl.BlockSpec((D_MODEL, TN), lambda i, j: (0, j))
    return pl.pallas_call(
        _merge_kernel,
        grid=(t // TM, D_MODEL // TN),
        in_specs=[row, row, row, wspec, wspec, wspec, wspec],
        out_specs=pl.BlockSpec((TM, TN), lambda i, j: (i, j)),
        out_shape=jax.ShapeDtypeStruct((t, D_MODEL), BF16),
        compiler_params=_params(("parallel", "arbitrary")),
        name="merge",
    )(u, cv, yn, wgc, wco, wgs, wso)


def _oproj_kernel(m_ref, h_ref, mod_ref, wo_ref, o_ref):
    o_ref[...] = h_ref[...] + mod_ref[0, 5:6, :] * _dot(m_ref[...], wo_ref[...])


def _oproj_call(m, h, mod, wo):
    t = m.shape[0]
    tpm = t // mod.shape[0] // TM
    tile = pl.BlockSpec((TM, TN), lambda i, j: (i, j))
    return pl.pallas_call(
        _oproj_kernel,
        grid=(t // TM, D_MODEL // TN),
        in_specs=[
            pl.BlockSpec((TM, D_MODEL), lambda i, j: (i, 0)),
            tile,
            pl.BlockSpec((1, N_MOD, TN), lambda i, j: (i // tpm, 0, j)),
            pl.BlockSpec((D_MODEL, TN), lambda i, j: (0, j)),
        ],
        out_specs=tile,
        out_shape=jax.ShapeDtypeStruct((t, D_MODEL), F32),
        compiler_params=_params(("parallel", "arbitrary")),
        name="out_proj",
    )(m, h, mod, wo)


def _group_lanes(f, b):
    lead = f.shape[:-1]
    fb = jnp.concatenate([f.reshape(lead + (SSM_GROUPS, HEADS_PER_GROUP)),
                          b.reshape(lead + (SSM_GROUPS, HEADS_PER_GROUP))], axis=-1)
    pad = [(0, 0)] * (fb.ndim - 1) + [(0, LANES - 2 * HEADS_PER_GROUP)]
    return jnp.pad(fb, pad).reshape(lead + (SSM_GROUPS * LANES,))


def _trunk_layer(h, mod, wts, *, seq_len, row_len, h0f, h0b):
    h1, u2 = _ffn_call(h, mod, wts["norm1"], wts["f1g"], wts["f1u"], wts["f1d"], wts["norm2"],
                       mod_base=0, final=False)
    cv = _convproj_call(u2, wts["w_cb"], wts["w_cc"], wts["w_cx"], wts["conv_w"], row_len=row_len)
    xbc, dt = _xbcproj_call(u2, wts["w_xbc"], wts["w_dt"], wts["dt_bias"])
    ssd = _ssd_call(xbc, dt, wts["ssm_conv_w"], wts["ssm_conv_b"], wts["a_log"], wts["d_skip"],
                    h0f, h0b, seq_len=seq_len)
    yn = _gatednorm_call(u2, ssd[0], wts["w_z"], wts["ssm_norm"])
    merged = _merge_call(u2, cv, yn, wts["w_gc"], wts["w_co"], wts["w_gs"], wts["w_so"])
    h2 = _oproj_call(merged, h1, mod, wts["w_o"])
    return h2, ssd[1:]


def kernel(x_prompt, x_sample, c, state_ssm_fwd, state_ssm_bwd, c_ctx, w_ada, b_ada, norm1_w, ffn1_w_gate, ffn1_w_up, ffn1_w_down, norm2_w, w_in, conv_w, ssm_conv_w, ssm_conv_b, dt_bias_f, dt_bias_b, a_log_f, a_log_b, d_skip, ssm_norm_w, w_conv_out, w_ssm_out, w_o, norm3_w, ffn2_w_gate, ffn2_w_up, ffn2_w_down, final_norm_w):
    depth = w_ada.shape[0]
    assert depth == 1, "the second ffn call is fused with the final norm, so exactly one layer is supported"
    nb, n_ctx, d = x_prompt.shape
    ndec, n_lat, _ = x_sample.shape
    hp = x_prompt.reshape(nb * n_ctx, d)
    hs = x_sample.reshape(ndec * n_lat, d)
    cvec = jnp.zeros((8, d), F32).at[0].set(c_ctx).at[1:1 + ndec].set(c)
    vec = lambda v: v.reshape(1, -1)
    bf = lambda w: w.astype(BF16)
    new_f, new_b = [], []
    for l in range(depth):
        mod = _mod_call(cvec, w_ada[l], b_ada[l]).reshape(8, N_MOD, d)
        wi = w_in[l]
        o = 0
        cols = {}
        for name, width in (("w_cb", CONV_W), ("w_cc", CONV_W), ("w_cx", CONV_W), ("w_z", SSM_INNER),
                            ("w_xbc", XBC_W), ("dtf", SSM_HEADS), ("dtb", SSM_HEADS),
                            ("w_gc", D_MODEL), ("w_gs", D_MODEL)):
            cols[name] = wi[:, o:o + width]
            o += width
        wts = {k: bf(cols[k]) for k in ("w_cb", "w_cc", "w_cx", "w_z", "w_xbc", "w_gc", "w_gs")}
        wts.update(
            w_dt=bf(_group_lanes(cols["dtf"], cols["dtb"])),
            dt_bias=vec(_group_lanes(dt_bias_f[l], dt_bias_b[l])),
            a_log=vec(_group_lanes(a_log_f[l], a_log_b[l])),
            d_skip=vec(jnp.repeat(d_skip[l], SSM_HEADDIM)),
            norm1=vec(norm1_w[l]), norm2=vec(norm2_w[l]), ssm_norm=vec(ssm_norm_w[l]),
            f1g=bf(ffn1_w_gate[l]), f1u=bf(ffn1_w_up[l]), f1d=bf(ffn1_w_down[l]),
            conv_w=conv_w[l], ssm_conv_w=ssm_conv_w[l], ssm_conv_b=vec(ssm_conv_b[l]),
            w_co=bf(w_conv_out[l]), w_so=bf(w_ssm_out[l]), w_o=bf(w_o[l]),
        )
        f2 = (vec(norm3_w[l]), bf(ffn2_w_gate[l]), bf(ffn2_w_up[l]), bf(ffn2_w_down[l]), vec(final_norm_w))
        pair = lambda s: s[:, l].reshape(ndec, SSM_HEADS // 2, 2 * SSM_HEADDIM, SSM_STATE)

        hp2, (hf, hb) = _trunk_layer(hp, mod[0:1], wts, seq_len=n_ctx, row_len=n_ctx, h0f=None, h0b=None)
        hs2, _ = _trunk_layer(hs, mod[1:1 + ndec], wts, seq_len=n_lat, row_len=GRID_W,
                              h0f=pair(state_ssm_fwd), h0b=pair(state_ssm_bwd))
        new_f.append(hf.reshape(nb, SSM_HEADS, SSM_HEADDIM, SSM_STATE))
        new_b.append(hb.reshape(nb, SSM_HEADS, SSM_HEADDIM, SSM_STATE))
        hp = _ffn_call(hp2, mod[0:1], *f2, mod_base=6, final=True)[0]
        hs = _ffn_call(hs2, mod[1:1 + ndec], *f2, mod_base=6, final=True)[0]
    return (hp.reshape(nb, n_ctx, d), hs.reshape(ndec, n_lat, d),
            jnp.stack(new_f, axis=1), jnp.stack(new_b, axis=1))
```

```python
import functools

import jax
import jax.numpy as jnp
from jax import lax
from jax.experimental import pallas as pl
from jax.experimental.pallas import tpu as pltpu

D_MODEL = 2048
D_FF = 5632
GRID_W = 64
CONV_W = 2048
SSM_INNER = 2048
SSM_HEADDIM = 64
SSM_HEADS = SSM_INNER // SSM_HEADDIM
SSM_GROUPS = 4
SSM_STATE = 128
CHUNK = 128
N_MOD = 9
EPS = 1e-6
XBC_W = SSM_INNER + 2 * SSM_GROUPS * SSM_STATE

HEADS_PER_GROUP = SSM_HEADS // SSM_GROUPS
PAIRS_PER_GROUP = HEADS_PER_GROUP // 2
GROUP_W = HEADS_PER_GROUP * SSM_HEADDIM
LANES = 128
NEG_BIG = -1e30
LOG2_E = 1.4426950408889634

TM = 512
TN = 512
TF = 512
TM_1 = 1024
TN_1 = 1024
TN_ADA = 1024
SSD_ROWS = 1024
VMEM_LIMIT = 52 * 1024 * 1024

F32 = jnp.float32
BF16 = jnp.bfloat16


def _dot(a, b):
    return jnp.dot(a, b, preferred_element_type=F32)


def _dot_nt(a, b):
    return lax.dot_general(a, b, (((1,), (1,)), ((), ())), preferred_element_type=F32)


def _silu(x):
    return x * jax.nn.sigmoid(x)


def _rms(x, w):
    ms = jnp.mean(x * x, axis=-1, keepdims=True)
    return x * lax.rsqrt(ms + EPS) * w


def _params(sem):
    return pltpu.CompilerParams(dimension_semantics=sem, vmem_limit_bytes=VMEM_LIMIT)


def _mod_kernel(c_ref, w_ref, b_ref, o_ref):
    a = _silu(c_ref[...]).astype(BF16)
    o_ref[...] = _dot(a, w_ref[...].astype(BF16)) + b_ref[...]


def _mod_call(cvec, w_ada, b_ada):
    n = w_ada.shape[1]
    return pl.pallas_call(
        _mod_kernel,
        grid=(n // TN_ADA,),
        in_specs=[
            pl.BlockSpec((8, D_MODEL), lambda j: (0, 0)),
            pl.BlockSpec((D_MODEL, TN_ADA), lambda j: (0, j)),
            pl.BlockSpec((1, TN_ADA), lambda j: (0, j)),
        ],
        out_specs=pl.BlockSpec((8, TN_ADA), lambda j: (0, j)),
        out_shape=jax.ShapeDtypeStruct((8, n), F32),
        compiler_params=_params(("arbitrary",)),
        name="adaln_mod",
    )(cvec, w_ada, b_ada.reshape(1, n))


def _ffn_kernel(h_ref, mod_ref, nw_ref, wg_ref, wu_ref, wd_ref, nw2_ref, *rest, mod_base, final):
    if final:
        o_ref, u_scr, acc_scr = rest
    else:
        o_ref, u_ref, u_scr, acc_scr = rest
    j = pl.program_id(1)
    nj = pl.num_programs(1)

    @pl.when(j == 0)
    def _():
        shift = mod_ref[0, mod_base:mod_base + 1, :]
        scale = mod_ref[0, mod_base + 1:mod_base + 2, :]
        u = _rms(h_ref[...], nw_ref[...]) * (1.0 + scale) + shift
        u_scr[...] = u.astype(BF16)
        acc_scr[...] = jnp.zeros_like(acc_scr)

    u = u_scr[...]
    a = _silu(_dot(u, wg_ref[...])) * _dot(u, wu_ref[...])
    acc_scr[...] += _dot(a.astype(BF16), wd_ref[...])

    @pl.when(j == nj - 1)
    def _():
        gate = mod_ref[0, mod_base + 2:mod_base + 3, :]
        hn = h_ref[...] + 0.5 * gate * acc_scr[...]
        if final:
            o_ref[...] = _rms(hn, nw2_ref[...])
        else:
            o_ref[...] = hn
            shift = mod_ref[0, mod_base + 3:mod_base + 4, :]
            scale = mod_ref[0, mod_base + 4:mod_base + 5, :]
            u_ref[...] = (_rms(hn, nw2_ref[...]) * (1.0 + scale) + shift).astype(BF16)


def _ffn_call(h, mod, nw, wg, wu, wd, nw2, *, mod_base, final):
    t = h.shape[0]
    tpm = t // mod.shape[0] // TM
    row = pl.BlockSpec((TM, D_MODEL), lambda i, j: (i, 0))
    vec = pl.BlockSpec((1, D_MODEL), lambda i, j: (0, 0))
    out_shape = [jax.ShapeDtypeStruct((t, D_MODEL), F32)]
    out_specs = [row]
    if not final:
        out_shape.append(jax.ShapeDtypeStruct((t, D_MODEL), BF16))
        out_specs.append(row)
    return pl.pallas_call(
        functools.partial(_ffn_kernel, mod_base=mod_base, final=final),
        grid=(t // TM, D_FF // TF),
        in_specs=[
            row,
            pl.BlockSpec((1, N_MOD, D_MODEL), lambda i, j: (i // tpm, 0, 0)),
            vec,
            pl.BlockSpec((D_MODEL, TF), lambda i, j: (0, j)),
            pl.BlockSpec((D_MODEL, TF), lambda i, j: (0, j)),
            pl.BlockSpec((TF, D_MODEL), lambda i, j: (j, 0)),
            vec,
        ],
        out_specs=out_specs,
        out_shape=out_shape,
        scratch_shapes=[pltpu.VMEM((TM, D_MODEL), BF16), pltpu.VMEM((TM, D_MODEL), F32)],
        compiler_params=_params(("parallel", "arbitrary")),
        name="ffn_final" if final else "ffn",
    )(h, mod, nw, wg, wu, wd, nw2)


def _convproj_kernel(u_ref, wb_ref, wc_ref, wx_ref, cw_ref, o_ref, *, row_len):
    u = u_ref[...]
    v = _dot(u, wc_ref[...]) * _dot(u, wx_ref[...])
    pos = lax.broadcasted_iota(jnp.int32, v.shape, 0) & (row_len - 1)
    v_prev = jnp.where(pos == 0, 0.0, pltpu.roll(v, 1, 0))
    v_next = jnp.where(pos == row_len - 1, 0.0, pltpu.roll(v, TM - 1, 0))
    cw = cw_ref[...]
    conv = cw[0:1, :] * v_prev + cw[1:2, :] * v + cw[2:3, :] * v_next
    o_ref[...] = (_dot(u, wb_ref[...]) * conv).astype(BF16)


def _convproj_call(u, wb, wc, wx, conv_w, *, row_len):
    t = u.shape[0]
    wspec = pl.BlockSpec((D_MODEL, TN), lambda i, j: (0, j))
    return pl.pallas_call(
        functools.partial(_convproj_kernel, row_len=row_len),
        grid=(t // TM, CONV_W // TN),
        in_specs=[
            pl.BlockSpec((TM, D_MODEL), lambda i, j: (i, 0)),
            wspec, wspec, wspec,
            pl.BlockSpec((3, TN), lambda i, j: (0, j)),
        ],
        out_specs=pl.BlockSpec((TM, TN), lambda i, j: (i, j)),
        out_shape=jax.ShapeDtypeStruct((t, CONV_W), BF16),
        compiler_params=_params(("parallel", "arbitrary")),
        name="conv_proj",
    )(u, wb, wc, wx, conv_w)


def _softplus(x):
    return jnp.maximum(x, 0.0) + jnp.log1p(jnp.exp(-jnp.abs(x)))


def _xbcproj_kernel(u_ref, w_ref, cw_ref, cb_ref, o_ref, *, seq_len):
    v = _dot(u_ref[...], w_ref[...])
    rows = v.shape[0]
    pos = lax.broadcasted_iota(jnp.int32, v.shape, 0) & (seq_len - 1)
    v_prev = jnp.where(pos == 0, 0.0, pltpu.roll(v, 1, 0))
    v_next = jnp.where(pos == seq_len - 1, 0.0, pltpu.roll(v, rows - 1, 0))
    cw = cw_ref[...]
    o_ref[...] = _silu(cw[0:1, :] * v_prev + cw[1:2, :] * v + cw[2:3, :] * v_next + cb_ref[...])


def _xbcproj_call(u, w_xbc, conv_w, conv_b, *, seq_len):
    t = u.shape[0]
    tm = max(TM_1, seq_len)
    tn = TN_1 * TM_1 // tm
    assert tm % seq_len == 0 and seq_len & (seq_len - 1) == 0
    return pl.pallas_call(
        functools.partial(_xbcproj_kernel, seq_len=seq_len),
        grid=(t // tm, XBC_W // tn),
        in_specs=[
            pl.BlockSpec((tm, D_MODEL), lambda i, j: (i, 0)),
            pl.BlockSpec((D_MODEL, tn), lambda i, j: (0, j)),
            pl.BlockSpec((3, tn), lambda i, j: (0, j)),
            pl.BlockSpec((1, tn), lambda i, j: (0, j)),
        ],
        out_specs=pl.BlockSpec((tm, tn), lambda i, j: (i, j)),
        out_shape=jax.ShapeDtypeStruct((t, XBC_W), F32),
        compiler_params=_params(("parallel", "arbitrary")),
        name="xbc_proj",
    )(u, w_xbc, conv_w, conv_b)


def _dtproj_kernel(u_ref, w_ref, b_ref, o_ref):
    o_ref[...] = _softplus(_dot(u_ref[...], w_ref[...]) + b_ref[...])


def _dtproj_call(u, w_dt, dt_bias):
    t = u.shape[0]
    ndt = w_dt.shape[1]
    return pl.pallas_call(
        _dtproj_kernel,
        grid=(t // TM_1,),
        in_specs=[
            pl.BlockSpec((TM_1, D_MODEL), lambda i: (i, 0)),
            pl.BlockSpec((D_MODEL, ndt), lambda i: (0, 0)),
            pl.BlockSpec((1, ndt), lambda i: (0, 0)),
        ],
        out_specs=pl.BlockSpec((TM_1, ndt), lambda i: (i, 0)),
        out_shape=jax.ShapeDtypeStruct((t, ndt), F32),
        compiler_params=_params(("parallel",)),
        name="dt_proj",
    )(u, w_dt, dt_bias)


def _split3(a):
    a1 = a.astype(BF16)
    r1 = a - a1.astype(F32)
    a2 = r1.astype(BF16)
    a3 = (r1 - a2.astype(F32)).astype(BF16)
    return a1, a2, a3


def _ssd_kernel(x_ref, b_ref, c_ref, dt_ref, alog_ref, dsk_ref, *rest, n_seq, seq_len, has_h0):
    if has_h0:
        h0f_ref, h0b_ref, y_ref, *scratch = rest
    else:
        y_ref, hf_ref, hb_ref, *scratch = rest
    x_st, b_bf, c_bf, cum_t, dt_t, w_t, dec, st_f, st_b = scratch
    q = CHUNK
    cps = seq_len // q
    nc = n_seq * cps
    hpg = HEADS_PER_GROUP
    nh = 2 * hpg

    lane = lax.broadcasted_iota(jnp.int32, (q, LANES), 1)
    row = lax.broadcasted_iota(jnp.int32, (q, q), 0)
    col = lax.broadcasted_iota(jnp.int32, (q, q), 1)
    tril = row >= col
    triu = row <= col
    lo_half = lane < SSM_HEADDIM
    lo_half_row = lo_half[0:1, :]

    def rows(c):
        return pl.ds(pl.multiple_of(c * q, q), q)

    def pair(p):
        return slice(p * LANES, (p + 1) * LANES)

    def cumsums(c, carry):
        r = rows(c)
        dt = dt_ref[r, :]
        a_row = jnp.where(lane[0:1, :] < nh, -LOG2_E * jnp.exp(alog_ref[...]), 0.0)
        a = dt * a_row
        tril_b = jnp.where(tril, 1.0, 0.0).astype(BF16)
        p3 = _dot(tril_b, jnp.concatenate(_split3(a), axis=1))
        pre = p3[:, 0:LANES] + p3[:, LANES:2 * LANES] + p3[:, 2 * LANES:]
        suf = (pre[q - 1:q, :] - pre) + a
        cm_t = jnp.where(lane < hpg, pre, suf).T[0:nh, :]
        dtt = dt.T[0:nh, :]
        cum_t[c] = cm_t
        dt_t[c] = dtt

        is_fwd = lax.broadcasted_iota(jnp.int32, (nh, q), 0) < hpg
        edge = jnp.where(is_fwd, cm_t[:, q - 1:q], cm_t[:, 0:1])
        w_t[c] = jnp.exp2(edge - cm_t) * dtt
        dec[c] = jnp.exp2(edge)
        return carry

    lax.fori_loop(0, nc, cumsums, 0, unroll=4)

    def own_states(c, carry):
        r = rows(c)
        x = x_ref[r, :]
        bm = b_ref[r, :]
        y_ref[r, :] = dsk_ref[...] * x
        lo_x = (lax.broadcasted_iota(jnp.int32, x.shape, 1) & (LANES - 1)) < SSM_HEADDIM
        x2 = jnp.concatenate([jnp.where(lo_x, x, 0.0), jnp.where(lo_x, 0.0, x)], axis=0).astype(BF16)
        x_st[c] = x2
        b_bf[r, :] = bm.astype(BF16)
        c_bf[r, :] = c_ref[r, :].astype(BF16)
        w = w_t[c]
        b_t = bm.T
        for p in range(PAIRS_PER_GROUP):
            lhs = jnp.concatenate(
                [jnp.concatenate([b_t * w[h:h + 1, :], b_t * w[h + 1:h + 2, :]], axis=1)
                 for h in (2 * p, hpg + 2 * p)], axis=0).astype(BF16)
            own = _dot(lhs, x2[:, pair(p)])
            st_f[c, :, pair(p)] = own[0:SSM_STATE, :]
            st_b[c, :, pair(p)] = own[SSM_STATE:, :]
        return carry

    lax.fori_loop(0, nc, own_states, 0, unroll=2)

    def pair_decay(c, h):
        d = dec[c]
        return jnp.where(lo_half_row, d[h:h + 1, :], d[h + 1:h + 2, :])

    for p in range(PAIRS_PER_GROUP):
        for s in range(n_seq):
            first = s * cps
            if has_h0:
                init = (h0f_ref[s, p].T, h0b_ref[s, p].T)
            else:
                init = (jnp.zeros((SSM_STATE, LANES), F32), jnp.zeros((SSM_STATE, LANES), F32))

            def chain(k, carry, p=p, first=first):
                sf, sb = carry
                cf = first + k
                cb = first + cps - 1 - k
                own_f = st_f[cf, :, pair(p)]
                own_b = st_b[cb, :, pair(p)]
                st_f[cf, :, pair(p)] = sf
                st_b[cb, :, pair(p)] = sb
                return (pair_decay(cf, 2 * p) * sf + own_f, pair_decay(cb, hpg + 2 * p) * sb + own_b)

            sf, sb = lax.fori_loop(0, cps, chain, init)
            if not has_h0:
                hf_ref[s, p] = sf.T
                hb_ref[s, p] = sb.T

    def main(c, carry):
        r = rows(c)
        cm_t = cum_t[c]
        dtt = dt_t[c]
        cb = c_bf[r, :]
        g = _dot_nt(cb, b_bf[r, :])
        off_f = _dot(cb, st_f[c].astype(BF16))
        off_b = _dot(cb, st_b[c].astype(BF16))
        for p in range(PAIRS_PER_GROUP):
            ms, col_f, col_b = [], [], []
            for e in range(2):
                hf = 2 * p + e
                hb = hpg + hf
                ci_f = jnp.broadcast_to(cm_t[hf:hf + 1, :], (q, q)).T
                ci_b = jnp.broadcast_to(cm_t[hb:hb + 1, :], (q, q)).T
                m_f = jnp.exp2(jnp.where(tril, ci_f - cm_t[hf:hf + 1, :], NEG_BIG)) * dtt[hf:hf + 1, :]
                m_b = jnp.exp2(jnp.where(triu, ci_b - cm_t[hb:hb + 1, :], NEG_BIG)) * dtt[hb:hb + 1, :]
                ms.append(((m_f + m_b) * g).astype(BF16))
                col_f.append(ci_f)
                col_b.append(ci_b)
            e_f = jnp.exp2(jnp.where(lo_half, col_f[0], col_f[1]))
            e_b = jnp.exp2(jnp.where(lo_half, col_b[0], col_b[1]))
            y_ref[r, pair(p)] += (_dot(jnp.concatenate(ms, axis=1), x_st[c, :, pair(p)])
                                  + e_f * off_f[:, pair(p)] + e_b * off_b[:, pair(p)])
        return carry

    lax.fori_loop(0, nc, main, 0, unroll=2)


def _ssd_call(xbc, dt, alog, dskip, h0f, h0b, *, seq_len):
    t = xbc.shape[0]
    n_seq = max(1, SSD_ROWS // seq_len)
    blk = n_seq * seq_len
    nc = blk // CHUNK
    has_h0 = h0f is not None
    gx = SSM_INNER // LANES
    gc = gx + SSM_GROUPS
    st_spec = pl.BlockSpec((n_seq, PAIRS_PER_GROUP, LANES, SSM_STATE), lambda s, g: (s, g, 0, 0))
    in_specs = [
        pl.BlockSpec((blk, GROUP_W), lambda s, g: (s, g)),
        pl.BlockSpec((blk, LANES), lambda s, g: (s, gx + g)),
        pl.BlockSpec((blk, LANES), lambda s, g: (s, gc + g)),
        pl.BlockSpec((blk, LANES), lambda s, g: (s, g)),
        pl.BlockSpec((1, LANES), lambda s, g: (0, g)),
        pl.BlockSpec((1, GROUP_W), lambda s, g: (0, g)),
    ]
    args = [xbc, xbc, xbc, dt, alog, dskip]
    y_shape = jax.ShapeDtypeStruct((t, SSM_INNER), F32)
    y_spec = pl.BlockSpec((blk, GROUP_W), lambda s, g: (s, g))
    st_shape = jax.ShapeDtypeStruct((t // seq_len, SSM_HEADS // 2, LANES, SSM_STATE), F32)
    if has_h0:
        in_specs += [st_spec, st_spec]
        args += [h0f, h0b]
        out_specs, out_shape = [y_spec], [y_shape]
    else:
        out_specs, out_shape = [y_spec, st_spec, st_spec], [y_shape, st_shape, st_shape]
    chunk_states = (nc, SSM_STATE, GROUP_W)
    return pl.pallas_call(
        functools.partial(_ssd_kernel, n_seq=n_seq, seq_len=seq_len, has_h0=has_h0),
        grid=(t // blk, SSM_GROUPS),
        in_specs=in_specs,
        out_specs=out_specs,
        out_shape=out_shape,
        scratch_shapes=[
            pltpu.VMEM((nc, 2 * CHUNK, GROUP_W), BF16),
            pltpu.VMEM((blk, LANES), BF16),
            pltpu.VMEM((blk, LANES), BF16),
            pltpu.VMEM((nc, 2 * HEADS_PER_GROUP, CHUNK), F32),
            pltpu.VMEM((nc, 2 * HEADS_PER_GROUP, CHUNK), F32),
            pltpu.VMEM((nc, 2 * HEADS_PER_GROUP, CHUNK), F32),
            pltpu.VMEM((nc, 2 * HEADS_PER_GROUP, CHUNK), F32),
            pltpu.VMEM(chunk_states, F32),
            pltpu.VMEM(chunk_states, F32),
        ],
        compiler_params=_params(("parallel", "arbitrary")),
        name="ssd_latent" if has_h0 else "ssd_context",
    )(*args)


def _gatednorm_kernel(u_ref, y_ref, wz_ref, nw_ref, o_ref, t_scr, ss_scr):
    j = pl.program_id(1)
    nj = SSM_INNER // TN_1
    t = y_ref[...] * _silu(_dot(u_ref[...], wz_ref[...]))
    t_scr[j] = t
    part = jnp.sum(t * t, axis=-1, keepdims=True)

    @pl.when(j == 0)
    def _():
        ss_scr[...] = part

    @pl.when(j > 0)
    def _():
        ss_scr[...] += part

    @pl.when(j == nj - 1)
    def _():
        r = lax.rsqrt(ss_scr[...] * (1.0 / SSM_INNER) + EPS)
        for k in range(nj):
            sl = slice(k * TN_1, (k + 1) * TN_1)
            o_ref[:, sl] = (t_scr[k] * r * nw_ref[:, sl]).astype(BF16)


def _gatednorm_call(u, y, wz, nw):
    t = u.shape[0]
    nj = SSM_INNER // TN_1
    return pl.pallas_call(
        _gatednorm_kernel,
        grid=(t // TM_1, nj),
        in_specs=[
            pl.BlockSpec((TM_1, D_MODEL), lambda i, j: (i, 0)),
            pl.BlockSpec((TM_1, TN_1), lambda i, j: (i, j)),
            pl.BlockSpec((D_MODEL, TN_1), lambda i, j: (0, j)),
            pl.BlockSpec((1, SSM_INNER), lambda i, j: (0, 0)),
        ],
        out_specs=pl.BlockSpec((TM_1, SSM_INNER), lambda i, j: (i, 0)),
        out_shape=jax.ShapeDtypeStruct((t, SSM_INNER), BF16),
        scratch_shapes=[pltpu.VMEM((nj, TM_1, TN_1), F32), pltpu.VMEM((TM_1, 1), F32)],
        compiler_params=_params(("parallel", "arbitrary")),
        name="gated_norm",
    )(u, y, wz, nw)


def _merge_kernel(u_ref, cv_ref, yn_ref, wgc_ref, wco_ref, wgs_ref, wso_ref, o_ref):
    u = u_ref[...]
    m = jax.nn.sigmoid(_dot(u, wgc_ref[...])) * _dot(cv_ref[...], wco_ref[...])
    m += jax.nn.sigmoid(_dot(u, wgs_ref[...])) * _dot(yn_ref[...], wso_ref[...])
    o_ref[...] = m.astype(BF16)


def _merge_call(u, cv, yn, wgc, wco, wgs, wso):
    t = u.shape[0]
    row = pl.BlockSpec((TM, D_MODEL), lambda i, j: (i, 0))
    wspec = pl.BlockSpec((D_MODEL, TN), lambda i, j: (0, j))
    return pl.pallas_call(
        _merge_kernel,
        grid=(t // TM, D_MODEL // TN),
        in_specs=[row, row, row, wspec, wspec, wspec, wspec],
        out_specs=pl.BlockSpec((TM, TN), lambda i, j: (i, j)),
        out_shape=jax.ShapeDtypeStruct((t, D_MODEL), BF16),
        compiler_params=_params(("parallel", "arbitrary")),
        name="merge",
    )(u, cv, yn, wgc, wco, wgs, wso)


def _oproj_kernel(m_ref, h_ref, mod_ref, wo_ref, o_ref):
    o_ref[...] = h_ref[...] + mod_ref[0, 5:6, :] * _dot(m_ref[...], wo_ref[...])


def _oproj_call(m, h, mod, wo):
    t = m.shape[0]
    tpm = t // mod.shape[0] // TM_1
    tile = pl.BlockSpec((TM_1, TN_1), lambda i, j: (i, j))
    return pl.pallas_call(
        _oproj_kernel,
        grid=(t // TM_1, D_MODEL // TN_1),
        in_specs=[
            pl.BlockSpec((TM_1, D_MODEL), lambda i, j: (i, 0)),
            tile,
            pl.BlockSpec((1, N_MOD, TN_1), lambda i, j: (i // tpm, 0, j)),
            pl.BlockSpec((D_MODEL, TN_1), lambda i, j: (0, j)),
        ],
        out_specs=tile,
        out_shape=jax.ShapeDtypeStruct((t, D_MODEL), F32),
        compiler_params=_params(("parallel", "arbitrary")),
        name="out_proj",
    )(m, h, mod, wo)


def _group_lanes(f, b):
    lead = f.shape[:-1]
    fb = jnp.concatenate([f.reshape(lead + (SSM_GROUPS, HEADS_PER_GROUP)),
                          b.reshape(lead + (SSM_GROUPS, HEADS_PER_GROUP))], axis=-1)
    pad = [(0, 0)] * (fb.ndim - 1) + [(0, LANES - 2 * HEADS_PER_GROUP)]
    return jnp.pad(fb, pad).reshape(lead + (SSM_GROUPS * LANES,))


def _trunk_layer(h, mod, wts, *, seq_len, row_len, h0f, h0b):
    h1, u2 = _ffn_call(h, mod, wts["norm1"], wts["f1g"], wts["f1u"], wts["f1d"], wts["norm2"],
                       mod_base=0, final=False)
    cv = _convproj_call(u2, wts["w_cb"], wts["w_cc"], wts["w_cx"], wts["conv_w"], row_len=row_len)
    xbc = _xbcproj_call(u2, wts["w_xbc"], wts["ssm_conv_w"], wts["ssm_conv_b"], seq_len=seq_len)
    dt = _dtproj_call(u2, wts["w_dt"], wts["dt_bias"])
    ssd = _ssd_call(xbc, dt, wts["a_log"], wts["d_skip"], h0f, h0b, seq_len=seq_len)
    yn = _gatednorm_call(u2, ssd[0], wts["w_z"], wts["ssm_norm"])
    merged = _merge_call(u2, cv, yn, wts["w_gc"], wts["w_co"], wts["w_gs"], wts["w_so"])
    h2 = _oproj_call(merged, h1, mod, wts["w_o"])
    return h2, ssd[1:]


def kernel(x_prompt, x_sample, c, state_ssm_fwd, state_ssm_bwd, c_ctx, w_ada, b_ada, norm1_w, ffn1_w_gate, ffn1_w_up, ffn1_w_down, norm2_w, w_in, conv_w, ssm_conv_w, ssm_conv_b, dt_bias_f, dt_bias_b, a_log_f, a_log_b, d_skip, ssm_norm_w, w_conv_out, w_ssm_out, w_o, norm3_w, ffn2_w_gate, ffn2_w_up, ffn2_w_down, final_norm_w):
    depth = w_ada.shape[0]
    assert depth == 1, "the second ffn call is fused with the final norm, so exactly one layer is supported"
    nb, n_ctx, d = x_prompt.shape
    ndec, n_lat, _ = x_sample.shape
    hp = x_prompt.reshape(nb * n_ctx, d)
    hs = x_sample.reshape(ndec * n_lat, d)
    cvec = jnp.zeros((8, d), F32).at[0].set(c_ctx).at[1:1 + ndec].set(c)
    vec = lambda v: v.reshape(1, -1)
    bf = lambda w: w.astype(BF16)
    new_f, new_b = [], []
    for l in range(depth):
        mod = _mod_call(cvec, w_ada[l], b_ada[l]).reshape(8, N_MOD, d)
        wi = w_in[l]
        o = 0
        cols = {}
        for name, width in (("w_cb", CONV_W), ("w_cc", CONV_W), ("w_cx", CONV_W), ("w_z", SSM_INNER),
                            ("w_xbc", XBC_W), ("dtf", SSM_HEADS), ("dtb", SSM_HEADS),
                            ("w_gc", D_MODEL), ("w_gs", D_MODEL)):
            cols[name] = wi[:, o:o + width]
            o += width
        wts = {k: bf(cols[k]) for k in ("w_cb", "w_cc", "w_cx", "w_z", "w_xbc", "w_gc", "w_gs")}
        wts.update(
            w_dt=bf(_group_lanes(cols["dtf"], cols["dtb"])),
            dt_bias=vec(_group_lanes(dt_bias_f[l], dt_bias_b[l])),
            a_log=vec(_group_lanes(a_log_f[l], a_log_b[l])),
            d_skip=vec(jnp.repeat(d_skip[l], SSM_HEADDIM)),
            norm1=vec(norm1_w[l]), norm2=vec(norm2_w[l]), ssm_norm=vec(ssm_norm_w[l]),
            f1g=bf(ffn1_w_gate[l]), f1u=bf(ffn1_w_up[l]), f1d=bf(ffn1_w_down[l]),
            conv_w=conv_w[l], ssm_conv_w=ssm_conv_w[l], ssm_conv_b=vec(ssm_conv_b[l]),
            w_co=bf(w_conv_out[l]), w_so=bf(w_ssm_out[l]), w_o=bf(w_o[l]),
        )
        f2 = (vec(norm3_w[l]), bf(ffn2_w_gate[l]), bf(ffn2_w_up[l]), bf(ffn2_w_down[l]), vec(final_norm_w))
        pair = lambda s: s[:, l].reshape(ndec, SSM_HEADS // 2, 2 * SSM_HEADDIM, SSM_STATE)

        hp2, (hf, hb) = _trunk_layer(hp, mod[0:1], wts, seq_len=n_ctx, row_len=n_ctx, h0f=None, h0b=None)
        hs2, _ = _trunk_layer(hs, mod[1:1 + ndec], wts, seq_len=n_lat, row_len=GRID_W,
                              h0f=pair(state_ssm_fwd), h0b=pair(state_ssm_bwd))
        new_f.append(hf.reshape(nb, SSM_HEADS, SSM_HEADDIM, SSM_STATE))
        new_b.append(hb.reshape(nb, SSM_HEADS, SSM_HEADDIM, SSM_STATE))
        hp = _ffn_call(hp2, mod[0:1], *f2, mod_base=6, final=True)[0]
        hs = _ffn_call(hs2, mod[1:1 + ndec], *f2, mod_base=6, final=True)[0]
    return (hp.reshape(nb, n_ctx, d), hs.reshape(ndec, n_lat, d),
            jnp.stack(new_f, axis=1), jnp.stack(new_b, axis=1))
```

```python
import functools

import jax
import jax.numpy as jnp
from jax import lax
from jax.experimental import pallas as pl
from jax.experimental.pallas import tpu as pltpu

D_MODEL = 2048
D_FF = 5632
GRID_W = 64
CONV_W = 2048
SSM_INNER = 2048
SSM_HEADDIM = 64
SSM_HEADS = SSM_INNER // SSM_HEADDIM
SSM_GROUPS = 4
SSM_STATE = 128
CHUNK = 128
N_MOD = 9
EPS = 1e-6
XBC_W = SSM_INNER + 2 * SSM_GROUPS * SSM_STATE

IN_SPLITS = (("cb", CONV_W), ("cc", CONV_W), ("cx", CONV_W), ("z", SSM_INNER), ("xbc", XBC_W),
             ("dtf", SSM_HEADS), ("dtb", SSM_HEADS), ("gc", D_MODEL), ("gs", D_MODEL))
IN_OFFSET = {}
for _name, _width in IN_SPLITS:
    IN_OFFSET[_name] = sum(w for _, w in IN_SPLITS[:len(IN_OFFSET)])
IN_WIDTH = dict(IN_SPLITS)

HEADS_PER_GROUP = SSM_HEADS // SSM_GROUPS
PAIRS_PER_GROUP = HEADS_PER_GROUP // 2
GROUP_W = HEADS_PER_GROUP * SSM_HEADDIM
LANES = 128
NEG_BIG = -1e30
LOG2_E = 1.4426950408889634

TM = 512
TN = 512
TF = 512
TM_1 = 1024
TN_1 = 1024
TN_ADA = 1024
SSD_ROWS = 1024
VMEM_LIMIT = 52 * 1024 * 1024

F32 = jnp.float32
BF16 = jnp.bfloat16


def _dot(a, b):
    return jnp.dot(a, b, preferred_element_type=F32)


def _dot_nt(a, b):
    return lax.dot_general(a, b, (((1,), (1,)), ((), ())), preferred_element_type=F32)


def _silu(x):
    return x * jax.nn.sigmoid(x)


def _rms(x, w):
    ms = jnp.mean(x * x, axis=-1, keepdims=True)
    return x * lax.rsqrt(ms + EPS) * w


def _params(sem):
    return pltpu.CompilerParams(dimension_semantics=sem, vmem_limit_bytes=VMEM_LIMIT)


def _mod_kernel(c_ref, w_ref, b_ref, o_ref):
    a = _silu(c_ref[...]).astype(BF16)
    o_ref[...] = _dot(a, w_ref[...].astype(BF16)) + b_ref[...]


def _mod_call(cvec, w_ada, b_ada):
    n = w_ada.shape[1]
    return pl.pallas_call(
        _mod_kernel,
        grid=(n // TN_ADA,),
        in_specs=[
            pl.BlockSpec((8, D_MODEL), lambda j: (0, 0)),
            pl.BlockSpec((D_MODEL, TN_ADA), lambda j: (0, j)),
            pl.BlockSpec((1, TN_ADA), lambda j: (0, j)),
        ],
        out_specs=pl.BlockSpec((8, TN_ADA), lambda j: (0, j)),
        out_shape=jax.ShapeDtypeStruct((8, n), F32),
        compiler_params=_params(("arbitrary",)),
        name="adaln_mod",
    )(cvec, w_ada, b_ada.reshape(1, n))


def _ffn_kernel(h_ref, mod_ref, nw_ref, wg_ref, wu_ref, wd_ref, nw2_ref, *rest, mod_base, final):
    if final:
        o_ref, u_scr, acc_scr = rest
    else:
        o_ref, u_ref, u_scr, acc_scr = rest
    j = pl.program_id(1)
    nj = pl.num_programs(1)

    @pl.when(j == 0)
    def _():
        shift = mod_ref[0, mod_base:mod_base + 1, :]
        scale = mod_ref[0, mod_base + 1:mod_base + 2, :]
        u = _rms(h_ref[...], nw_ref[...]) * (1.0 + scale) + shift
        u_scr[...] = u.astype(BF16)
        acc_scr[...] = jnp.zeros_like(acc_scr)

    u = u_scr[...]
    a = _silu(_dot(u, wg_ref[...])) * _dot(u, wu_ref[...])
    acc_scr[...] += _dot(a.astype(BF16), wd_ref[...])

    @pl.when(j == nj - 1)
    def _():
        gate = mod_ref[0, mod_base + 2:mod_base + 3, :]
        hn = h_ref[...] + 0.5 * gate * acc_scr[...]
        if final:
            o_ref[...] = _rms(hn, nw2_ref[...])
        else:
            o_ref[...] = hn
            shift = mod_ref[0, mod_base + 3:mod_base + 4, :]
            scale = mod_ref[0, mod_base + 4:mod_base + 5, :]
            u_ref[...] = (_rms(hn, nw2_ref[...]) * (1.0 + scale) + shift).astype(BF16)


def _ffn_call(h, mod, nw, wg, wu, wd, nw2, *, mod_base, final):
    t = h.shape[0]
    tpm = t // mod.shape[0] // TM
    row = pl.BlockSpec((TM, D_MODEL), lambda i, j: (i, 0))
    vec = pl.BlockSpec((1, D_MODEL), lambda i, j: (0, 0))
    out_shape = [jax.ShapeDtypeStruct((t, D_MODEL), F32)]
    out_specs = [row]
    if not final:
        out_shape.append(jax.ShapeDtypeStruct((t, D_MODEL), BF16))
        out_specs.append(row)
    return pl.pallas_call(
        functools.partial(_ffn_kernel, mod_base=mod_base, final=final),
        grid=(t // TM, D_FF // TF),
        in_specs=[
            row,
            pl.BlockSpec((1, N_MOD, D_MODEL), lambda i, j: (i // tpm, 0, 0)),
            vec,
            pl.BlockSpec((D_MODEL, TF), lambda i, j: (0, j)),
            pl.BlockSpec((D_MODEL, TF), lambda i, j: (0, j)),
            pl.BlockSpec((TF, D_MODEL), lambda i, j: (j, 0)),
            vec,
        ],
        out_specs=out_specs,
        out_shape=out_shape,
        scratch_shapes=[pltpu.VMEM((TM, D_MODEL), BF16), pltpu.VMEM((TM, D_MODEL), F32)],
        compiler_params=_params(("parallel", "arbitrary")),
        name="ffn_final" if final else "ffn",
    )(h, mod, nw, wg, wu, wd, nw2)


def _convproj_kernel(u_ref, wb_ref, wc_ref, wx_ref, cw_ref, o_ref, *, row_len):
    u = u_ref[...]
    v = _dot_nt(u, wc_ref[...]) * _dot_nt(u, wx_ref[...])
    pos = lax.broadcasted_iota(jnp.int32, v.shape, 0) & (row_len - 1)
    v_prev = jnp.where(pos == 0, 0.0, pltpu.roll(v, 1, 0))
    v_next = jnp.where(pos == row_len - 1, 0.0, pltpu.roll(v, TM - 1, 0))
    cw = cw_ref[...]
    conv = cw[0:1, :] * v_prev + cw[1:2, :] * v + cw[2:3, :] * v_next
    o_ref[...] = (_dot_nt(u, wb_ref[...]) * conv).astype(BF16)


def _w_in_rows(name, tn):
    first = IN_OFFSET[name] // tn
    assert first * tn == IN_OFFSET[name]
    return pl.BlockSpec((tn, D_MODEL), lambda i, j: (first + j, 0))


def _convproj_call(u, w_in_t, conv_w, *, row_len):
    t = u.shape[0]
    return pl.pallas_call(
        functools.partial(_convproj_kernel, row_len=row_len),
        grid=(t // TM, CONV_W // TN),
        in_specs=[
            pl.BlockSpec((TM, D_MODEL), lambda i, j: (i, 0)),
            _w_in_rows("cb", TN), _w_in_rows("cc", TN), _w_in_rows("cx", TN),
            pl.BlockSpec((3, TN), lambda i, j: (0, j)),
        ],
        out_specs=pl.BlockSpec((TM, TN), lambda i, j: (i, j)),
        out_shape=jax.ShapeDtypeStruct((t, CONV_W), BF16),
        compiler_params=_params(("parallel", "arbitrary")),
        name="conv_proj",
    )(u, w_in_t, w_in_t, w_in_t, conv_w)


def _softplus(x):
    return jnp.maximum(x, 0.0) + jnp.log1p(jnp.exp(-jnp.abs(x)))


def _xbcproj_kernel(u_ref, w_ref, cw_ref, cb_ref, o_ref, *, seq_len):
    v = _dot_nt(u_ref[...], w_ref[...])
    rows = v.shape[0]
    pos = lax.broadcasted_iota(jnp.int32, v.shape, 0) & (seq_len - 1)
    v_prev = jnp.where(pos == 0, 0.0, pltpu.roll(v, 1, 0))
    v_next = jnp.where(pos == seq_len - 1, 0.0, pltpu.roll(v, rows - 1, 0))
    cw = cw_ref[...]
    o_ref[...] = _silu(cw[0:1, :] * v_prev + cw[1:2, :] * v + cw[2:3, :] * v_next + cb_ref[...])


def _xbcproj_call(u, w_in_t, conv_w, conv_b, *, seq_len):
    t = u.shape[0]
    tm = max(TM_1, seq_len)
    tn = TN_1 * TM_1 // tm
    assert tm % seq_len == 0 and seq_len & (seq_len - 1) == 0
    return pl.pallas_call(
        functools.partial(_xbcproj_kernel, seq_len=seq_len),
        grid=(t // tm, XBC_W // tn),
        in_specs=[
            pl.BlockSpec((tm, D_MODEL), lambda i, j: (i, 0)),
            _w_in_rows("xbc", tn),
            pl.BlockSpec((3, tn), lambda i, j: (0, j)),
            pl.BlockSpec((1, tn), lambda i, j: (0, j)),
        ],
        out_specs=pl.BlockSpec((tm, tn), lambda i, j: (i, j)),
        out_shape=jax.ShapeDtypeStruct((t, XBC_W), F32),
        compiler_params=_params(("parallel", "arbitrary")),
        name="xbc_proj",
    )(u, w_in_t, conv_w, conv_b)


def _dtproj_kernel(u_ref, w_ref, b_ref, o_ref):
    o_ref[...] = _softplus(_dot_nt(u_ref[...], w_ref[...]) + b_ref[...])


def _dtproj_call(u, w_dt_t, dt_bias):
    t = u.shape[0]
    ndt = w_dt_t.shape[0]
    return pl.pallas_call(
        _dtproj_kernel,
        grid=(t // TM_1,),
        in_specs=[
            pl.BlockSpec((TM_1, D_MODEL), lambda i: (i, 0)),
            pl.BlockSpec((ndt, D_MODEL), lambda i: (0, 0)),
            pl.BlockSpec((1, ndt), lambda i: (0, 0)),
        ],
        out_specs=pl.BlockSpec((TM_1, ndt), lambda i: (i, 0)),
        out_shape=jax.ShapeDtypeStruct((t, ndt), F32),
        compiler_params=_params(("parallel",)),
        name="dt_proj",
    )(u, w_dt_t, dt_bias)


def _split3(a):
    a1 = a.astype(BF16)
    r1 = a - a1.astype(F32)
    a2 = r1.astype(BF16)
    a3 = (r1 - a2.astype(F32)).astype(BF16)
    return a1, a2, a3


def _ssd_kernel(x_ref, b_ref, c_ref, dt_ref, alog_ref, dsk_ref, *rest, n_seq, seq_len, has_h0):
    if has_h0:
        h0f_ref, h0b_ref, y_ref, *scratch = rest
    else:
        y_ref, hf_ref, hb_ref, *scratch = rest
    x_st, b_bf, c_bf, cum_t, dt_t, w_t, dec, st_f, st_b = scratch
    q = CHUNK
    cps = seq_len // q
    nc = n_seq * cps
    hpg = HEADS_PER_GROUP
    nh = 2 * hpg

    lane = lax.broadcasted_iota(jnp.int32, (q, LANES), 1)
    row = lax.broadcasted_iota(jnp.int32, (q, q), 0)
    col = lax.broadcasted_iota(jnp.int32, (q, q), 1)
    tril = row >= col
    triu = row <= col
    lo_half = lane < SSM_HEADDIM
    lo_half_row = lo_half[0:1, :]

    def rows(c):
        return pl.ds(pl.multiple_of(c * q, q), q)

    def pair(p):
        return slice(p * LANES, (p + 1) * LANES)

    def cumsums(c, carry):
        r = rows(c)
        dt = dt_ref[r, :]
        a_row = jnp.where(lane[0:1, :] < nh, -LOG2_E * jnp.exp(alog_ref[...]), 0.0)
        a = dt * a_row
        tril_b = jnp.where(tril, 1.0, 0.0).astype(BF16)
        p3 = _dot(tril_b, jnp.concatenate(_split3(a), axis=1))
        pre = p3[:, 0:LANES] + p3[:, LANES:2 * LANES] + p3[:, 2 * LANES:]
        suf = (pre[q - 1:q, :] - pre) + a
        cm_t = jnp.where(lane < hpg, pre, suf).T[0:nh, :]
        dtt = dt.T[0:nh, :]
        cum_t[c] = cm_t
        dt_t[c] = dtt

        is_fwd = lax.broadcasted_iota(jnp.int32, (nh, q), 0) < hpg
        edge = jnp.where(is_fwd, cm_t[:, q - 1:q], cm_t[:, 0:1])
        w_t[c] = jnp.exp2(edge - cm_t) * dtt
        dec[c] = jnp.exp2(edge)
        return carry

    lax.fori_loop(0, nc, cumsums, 0, unroll=4)

    def own_states(c, carry):
        r = rows(c)
        x = x_ref[r, :]
        bm = b_ref[r, :]
        y_ref[r, :] = dsk_ref[...] * x
        lo_x = (lax.broadcasted_iota(jnp.int32, x.shape, 1) & (LANES - 1)) < SSM_HEADDIM
        x2 = jnp.concatenate([jnp.where(lo_x, x, 0.0), jnp.where(lo_x, 0.0, x)], axis=0).astype(BF16)
        x_st[c] = x2
        b_bf[r, :] = bm.astype(BF16)
        c_bf[r, :] = c_ref[r, :].astype(BF16)
        w = w_t[c]
        b_t = bm.T
        for p in range(PAIRS_PER_GROUP):
            lhs = jnp.concatenate(
                [jnp.concatenate([b_t * w[h:h + 1, :], b_t * w[h + 1:h + 2, :]], axis=1)
                 for h in (2 * p, hpg + 2 * p)], axis=0).astype(BF16)
            own = _dot(lhs, x2[:, pair(p)])
            st_f[c, :, pair(p)] = own[0:SSM_STATE, :]
            st_b[c, :, pair(p)] = own[SSM_STATE:, :]
        return carry

    lax.fori_loop(0, nc, own_states, 0, unroll=2)

    def pair_decay(c, h):
        d = dec[c]
        return jnp.where(lo_half_row, d[h:h + 1, :], d[h + 1:h + 2, :])

    for p in range(PAIRS_PER_GROUP):
        for s in range(n_seq):
            first = s * cps
            if has_h0:
                init = (h0f_ref[s, p].T, h0b_ref[s, p].T)
            else:
                init = (jnp.zeros((SSM_STATE, LANES), F32), jnp.zeros((SSM_STATE, LANES), F32))

            def chain(k, carry, p=p, first=first):
                sf, sb = carry
                cf = first + k
                cb = first + cps - 1 - k
                own_f = st_f[cf, :, pair(p)]
                own_b = st_b[cb, :, pair(p)]
                st_f[cf, :, pair(p)] = sf
                st_b[cb, :, pair(p)] = sb
                return (pair_decay(cf, 2 * p) * sf + own_f, pair_decay(cb, hpg + 2 * p) * sb + own_b)

            sf, sb = lax.fori_loop(0, cps, chain, init)
            if not has_h0:
                hf_ref[s, p] = sf.T
                hb_ref[s, p] = sb.T

    def main(c, carry):
        r = rows(c)
        cm_t = cum_t[c]
        dtt = dt_t[c]
        cb = c_bf[r, :]
        g = _dot_nt(cb, b_bf[r, :])
        off_f = _dot(cb, st_f[c].astype(BF16))
        off_b = _dot(cb, st_b[c].astype(BF16))
        for p in range(PAIRS_PER_GROUP):
            ms, col_f, col_b = [], [], []
            for e in range(2):
                hf = 2 * p + e
                hb = hpg + hf
                ci_f = jnp.broadcast_to(cm_t[hf:hf + 1, :], (q, q)).T
                ci_b = jnp.broadcast_to(cm_t[hb:hb + 1, :], (q, q)).T
                m_f = jnp.exp2(jnp.where(tril, ci_f - cm_t[hf:hf + 1, :], NEG_BIG)) * dtt[hf:hf + 1, :]
                m_b = jnp.exp2(jnp.where(triu, ci_b - cm_t[hb:hb + 1, :], NEG_BIG)) * dtt[hb:hb + 1, :]
                ms.append(((m_f + m_b) * g).astype(BF16))
                col_f.append(ci_f)
                col_b.append(ci_b)
            e_f = jnp.exp2(jnp.where(lo_half, col_f[0], col_f[1]))
            e_b = jnp.exp2(jnp.where(lo_half, col_b[0], col_b[1]))
            y_ref[r, pair(p)] += (_dot(jnp.concatenate(ms, axis=1), x_st[c, :, pair(p)])
                                  + e_f * off_f[:, pair(p)] + e_b * off_b[:, pair(p)])
        return carry

    lax.fori_loop(0, nc, main, 0, unroll=2)


def _ssd_call(xbc, dt, alog, dskip, h0f, h0b, *, seq_len):
    t = xbc.shape[0]
    n_seq = max(1, SSD_ROWS // seq_len)
    blk = n_seq * seq_len
    nc = blk // CHUNK
    has_h0 = h0f is not None
    gx = SSM_INNER // LANES
    gc = gx + SSM_GROUPS
    st_spec = pl.BlockSpec((n_seq, PAIRS_PER_GROUP, LANES, SSM_STATE), lambda s, g: (s, g, 0, 0))
    in_specs = [
        pl.BlockSpec((blk, GROUP_W), lambda s, g: (s, g)),
        pl.BlockSpec((blk, LANES), lambda s, g: (s, gx + g)),
        pl.BlockSpec((blk, LANES), lambda s, g: (s, gc + g)),
        pl.BlockSpec((blk, LANES), lambda s, g: (s, g)),
        pl.BlockSpec((1, LANES), lambda s, g: (0, g)),
        pl.BlockSpec((1, GROUP_W), lambda s, g: (0, g)),
    ]
    args = [xbc, xbc, xbc, dt, alog, dskip]
    y_shape = jax.ShapeDtypeStruct((t, SSM_INNER), F32)
    y_spec = pl.BlockSpec((blk, GROUP_W), lambda s, g: (s, g))
    st_shape = jax.ShapeDtypeStruct((t // seq_len, SSM_HEADS // 2, LANES, SSM_STATE), F32)
    if has_h0:
        in_specs += [st_spec, st_spec]
        args += [h0f, h0b]
        out_specs, out_shape = [y_spec], [y_shape]
    else:
        out_specs, out_shape = [y_spec, st_spec, st_spec], [y_shape, st_shape, st_shape]
    chunk_states = (nc, SSM_STATE, GROUP_W)
    return pl.pallas_call(
        functools.partial(_ssd_kernel, n_seq=n_seq, seq_len=seq_len, has_h0=has_h0),
        grid=(t // blk, SSM_GROUPS),
        in_specs=in_specs,
        out_specs=out_specs,
        out_shape=out_shape,
        scratch_shapes=[
            pltpu.VMEM((nc, 2 * CHUNK, GROUP_W), BF16),
            pltpu.VMEM((blk, LANES), BF16),
            pltpu.VMEM((blk, LANES), BF16),
            pltpu.VMEM((nc, 2 * HEADS_PER_GROUP, CHUNK), F32),
            pltpu.VMEM((nc, 2 * HEADS_PER_GROUP, CHUNK), F32),
            pltpu.VMEM((nc, 2 * HEADS_PER_GROUP, CHUNK), F32),
            pltpu.VMEM((nc, 2 * HEADS_PER_GROUP, CHUNK), F32),
            pltpu.VMEM(chunk_states, F32),
            pltpu.VMEM(chunk_states, F32),
        ],
        compiler_params=_params(("parallel", "arbitrary")),
        name="ssd_latent" if has_h0 else "ssd_context",
    )(*args)


def _gatednorm_kernel(u_ref, y_ref, wz_ref, nw_ref, o_ref, t_scr, ss_scr):
    j = pl.program_id(1)
    nj = SSM_INNER // TN_1
    t = y_ref[...] * _silu(_dot_nt(u_ref[...], wz_ref[...]))
    t_scr[j] = t
    part = jnp.sum(t * t, axis=-1, keepdims=True)

    @pl.when(j == 0)
    def _():
        ss_scr[...] = part

    @pl.when(j > 0)
    def _():
        ss_scr[...] += part

    @pl.when(j == nj - 1)
    def _():
        r = lax.rsqrt(ss_scr[...] * (1.0 / SSM_INNER) + EPS)
        for k in range(nj):
            sl = slice(k * TN_1, (k + 1) * TN_1)
            o_ref[:, sl] = (t_scr[k] * r * nw_ref[:, sl]).astype(BF16)


def _gatednorm_call(u, y, w_in_t, nw):
    t = u.shape[0]
    nj = SSM_INNER // TN_1
    return pl.pallas_call(
        _gatednorm_kernel,
        grid=(t // TM_1, nj),
        in_specs=[
            pl.BlockSpec((TM_1, D_MODEL), lambda i, j: (i, 0)),
            pl.BlockSpec((TM_1, TN_1), lambda i, j: (i, j)),
            _w_in_rows("z", TN_1),
            pl.BlockSpec((1, SSM_INNER), lambda i, j: (0, 0)),
        ],
        out_specs=pl.BlockSpec((TM_1, SSM_INNER), lambda i, j: (i, 0)),
        out_shape=jax.ShapeDtypeStruct((t, SSM_INNER), BF16),
        scratch_shapes=[pltpu.VMEM((nj, TM_1, TN_1), F32), pltpu.VMEM((TM_1, 1), F32)],
        compiler_params=_params(("parallel", "arbitrary")),
        name="gated_norm",
    )(u, y, w_in_t, nw)


def _merge_kernel(u_ref, cv_ref, yn_ref, wgc_ref, wco_ref, wgs_ref, wso_ref, o_ref):
    u = u_ref[...]
    m = jax.nn.sigmoid(_dot_nt(u, wgc_ref[...])) * _dot(cv_ref[...], wco_ref[...])
    m += jax.nn.sigmoid(_dot_nt(u, wgs_ref[...])) * _dot(yn_ref[...], wso_ref[...])
    o_ref[...] = m.astype(BF16)


def _merge_call(u, cv, yn, wgc_t, wco, wgs_t, wso):
    t = u.shape[0]
    row = pl.BlockSpec((TM, D_MODEL), lambda i, j: (i, 0))
    wspec = pl.BlockSpec((D_MODEL, TN), lambda i, j: (0, j))
    wspec_t = pl.BlockSpec((TN, D_MODEL), lambda i, j: (j, 0))
    return pl.pallas_call(
        _merge_kernel,
        grid=(t // TM, D_MODEL // TN),
        in_specs=[row, row, row, wspec_t, wspec, wspec_t, wspec],
        out_specs=pl.BlockSpec((TM, TN), lambda i, j: (i, j)),
        out_shape=jax.ShapeDtypeStruct((t, D_MODEL), BF16),
        compiler_params=_params(("parallel", "arbitrary")),
        name="merge",
    )(u, cv, yn, wgc_t, wco, wgs_t, wso)


def _oproj_kernel(m_ref, h_ref, mod_ref, wo_ref, o_ref):
    o_ref[...] = h_ref[...] + mod_ref[0, 5:6, :] * _dot(m_ref[...], wo_ref[...])


def _oproj_call(m, h, mod, wo):
    t = m.shape[0]
    tpm = t // mod.shape[0] // TM_1
    tile = pl.BlockSpec((TM_1, TN_1), lambda i, j: (i, j))
    return pl.pallas_call(
        _oproj_kernel,
        grid=(t // TM_1, D_MODEL // TN_1),
        in_specs=[
            pl.BlockSpec((TM_1, D_MODEL), lambda i, j: (i, 0)),
            tile,
            pl.BlockSpec((1, N_MOD, TN_1), lambda i, j: (i // tpm, 0, j)),
            pl.BlockSpec((D_MODEL, TN_1), lambda i, j: (0, j)),
        ],
        out_specs=tile,
        out_shape=jax.ShapeDtypeStruct((t, D_MODEL), F32),
        compiler_params=_params(("parallel", "arbitrary")),
        name="out_proj",
    )(m, h, mod, wo)


def _group_lanes(f, b):
    lead = f.shape[:-1]
    fb = jnp.concatenate([f.reshape(lead + (SSM_GROUPS, HEADS_PER_GROUP)),
                          b.reshape(lead + (SSM_GROUPS, HEADS_PER_GROUP))], axis=-1)
    pad = [(0, 0)] * (fb.ndim - 1) + [(0, LANES - 2 * HEADS_PER_GROUP)]
    return jnp.pad(fb, pad).reshape(lead + (SSM_GROUPS * LANES,))


def _trunk_layer(h, mod, wts, *, seq_len, row_len, h0f, h0b):
    h1, u2 = _ffn_call(h, mod, wts["norm1"], wts["f1g"], wts["f1u"], wts["f1d"], wts["norm2"],
                       mod_base=0, final=False)
    cv = _convproj_call(u2, wts["w_in_t"], wts["conv_w"], row_len=row_len)
    xbc = _xbcproj_call(u2, wts["w_in_t"], wts["ssm_conv_w"], wts["ssm_conv_b"], seq_len=seq_len)
    dt = _dtproj_call(u2, wts["w_dt_t"], wts["dt_bias"])
    ssd = _ssd_call(xbc, dt, wts["a_log"], wts["d_skip"], h0f, h0b, seq_len=seq_len)
    yn = _gatednorm_call(u2, ssd[0], wts["w_in_t"], wts["ssm_norm"])
    merged = _merge_call(u2, cv, yn, wts["w_gc_t"], wts["w_co"], wts["w_gs_t"], wts["w_so"])
    h2 = _oproj_call(merged, h1, mod, wts["w_o"])
    return h2, ssd[1:]


def kernel(x_prompt, x_sample, c, state_ssm_fwd, state_ssm_bwd, c_ctx, w_ada, b_ada, norm1_w, ffn1_w_gate, ffn1_w_up, ffn1_w_down, norm2_w, w_in, conv_w, ssm_conv_w, ssm_conv_b, dt_bias_f, dt_bias_b, a_log_f, a_log_b, d_skip, ssm_norm_w, w_conv_out, w_ssm_out, w_o, norm3_w, ffn2_w_gate, ffn2_w_up, ffn2_w_down, final_norm_w):
    depth = w_ada.shape[0]
    assert depth == 1, "the second ffn call is fused with the final norm, so exactly one layer is supported"
    nb, n_ctx, d = x_prompt.shape
    ndec, n_lat, _ = x_sample.shape
    hp = x_prompt.reshape(nb * n_ctx, d)
    hs = x_sample.reshape(ndec * n_lat, d)
    cvec = jnp.zeros((8, d), F32).at[0].set(c_ctx).at[1:1 + ndec].set(c)
    vec = lambda v: v.reshape(1, -1)
    bf = lambda w: w.astype(BF16)
    new_f, new_b = [], []
    for l in range(depth):
        mod = _mod_call(cvec, w_ada[l], b_ada[l]).reshape(8, N_MOD, d)
        w_in_t = bf(w_in[l].T)
        rows = lambda name: w_in_t[IN_OFFSET[name]:IN_OFFSET[name] + IN_WIDTH[name]]
        wts = dict(
            w_in_t=w_in_t, w_gc_t=rows("gc"), w_gs_t=rows("gs"),
            w_dt_t=_group_lanes(rows("dtf").T, rows("dtb").T).T,
            dt_bias=vec(_group_lanes(dt_bias_f[l], dt_bias_b[l])),
            a_log=vec(_group_lanes(a_log_f[l], a_log_b[l])),
            d_skip=vec(jnp.repeat(d_skip[l], SSM_HEADDIM)),
            norm1=vec(norm1_w[l]), norm2=vec(norm2_w[l]), ssm_norm=vec(ssm_norm_w[l]),
            f1g=bf(ffn1_w_gate[l]), f1u=bf(ffn1_w_up[l]), f1d=bf(ffn1_w_down[l]),
            conv_w=conv_w[l], ssm_conv_w=ssm_conv_w[l], ssm_conv_b=vec(ssm_conv_b[l]),
            w_co=bf(w_conv_out[l]), w_so=bf(w_ssm_out[l]), w_o=bf(w_o[l]),
        )
        f2 = (vec(norm3_w[l]), bf(ffn2_w_gate[l]), bf(ffn2_w_up[l]), bf(ffn2_w_down[l]), vec(final_norm_w))
        pair = lambda s: s[:, l].reshape(ndec, SSM_HEADS // 2, 2 * SSM_HEADDIM, SSM_STATE)

        hp2, (hf, hb) = _trunk_layer(hp, mod[0:1], wts, seq_len=n_ctx, row_len=n_ctx, h0f=None, h0b=None)
        hs2, _ = _trunk_layer(hs, mod[1:1 + ndec], wts, seq_len=n_lat, row_len=GRID_W,
                              h0f=pair(state_ssm_fwd), h0b=pair(state_ssm_bwd))
        new_f.append(hf.reshape(nb, SSM_HEADS, SSM_HEADDIM, SSM_STATE))
        new_b.append(hb.reshape(nb, SSM_HEADS, SSM_HEADDIM, SSM_STATE))
        hp = _ffn_call(hp2, mod[0:1], *f2, mod_base=6, final=True)[0]
        hs = _ffn_call(hs2, mod[1:1 + ndec], *f2, mod_base=6, final=True)[0]
    return (hp.reshape(nb, n_ctx, d), hs.reshape(ndec, n_lat, d),
            jnp.stack(new_f, axis=1), jnp.stack(new_b, axis=1))
```

```python
import functools

import jax
import jax.numpy as jnp
from jax import lax
from jax.experimental import pallas as pl
from jax.experimental.pallas import tpu as pltpu

D_MODEL = 2048
D_FF = 5632
GRID_W = 64
CONV_W = 2048
SSM_INNER = 2048
SSM_HEADDIM = 64
SSM_HEADS = SSM_INNER // SSM_HEADDIM
SSM_GROUPS = 4
SSM_STATE = 128
CHUNK = 128
N_MOD = 9
EPS = 1e-6
XBC_W = SSM_INNER + 2 * SSM_GROUPS * SSM_STATE

IN_SPLITS = (("cb", CONV_W), ("cc", CONV_W), ("cx", CONV_W), ("z", SSM_INNER), ("xbc", XBC_W),
             ("dtf", SSM_HEADS), ("dtb", SSM_HEADS), ("gc", D_MODEL), ("gs", D_MODEL))
IN_OFFSET = {}
for _name, _width in IN_SPLITS:
    IN_OFFSET[_name] = sum(w for _, w in IN_SPLITS[:len(IN_OFFSET)])
IN_WIDTH = dict(IN_SPLITS)

HEADS_PER_GROUP = SSM_HEADS // SSM_GROUPS
PAIRS_PER_GROUP = HEADS_PER_GROUP // 2
GROUP_W = HEADS_PER_GROUP * SSM_HEADDIM
LANES = 128
NEG_BIG = -1e30
LOG2_E = 1.4426950408889634

TM = 512
TN = 512
TF = 512
TM_1 = 1024
TN_1 = 1024
TN_ADA = 1024
ROW_SLAB = 16
SSD_ROWS = 1024
VMEM_LIMIT = 52 * 1024 * 1024

F32 = jnp.float32
BF16 = jnp.bfloat16


def _dot(a, b):
    return jnp.dot(a, b, preferred_element_type=F32)


def _dot_nt(a, b):
    return lax.dot_general(a, b, (((1,), (1,)), ((), ())), preferred_element_type=F32)


def _silu(x):
    return x * jax.nn.sigmoid(x)


def _rms(x, w):
    ms = jnp.mean(x * x, axis=-1, keepdims=True)
    return x * lax.rsqrt(ms + EPS) * w


def _for_row_slabs(n_rows, fn):
    def body(s, carry):
        fn(pl.ds(pl.multiple_of(s * ROW_SLAB, ROW_SLAB), ROW_SLAB))
        return carry

    lax.fori_loop(0, n_rows // ROW_SLAB, body, 0, unroll=8)


def _params(sem):
    return pltpu.CompilerParams(dimension_semantics=sem, vmem_limit_bytes=VMEM_LIMIT)


def _mod_kernel(c_ref, w_ref, b_ref, o_ref):
    a = _silu(c_ref[...]).astype(BF16)
    o_ref[...] = _dot(a, w_ref[...].astype(BF16)) + b_ref[...]


def _mod_call(cvec, w_ada, b_ada):
    n = w_ada.shape[1]
    return pl.pallas_call(
        _mod_kernel,
        grid=(n // TN_ADA,),
        in_specs=[
            pl.BlockSpec((8, D_MODEL), lambda j: (0, 0)),
            pl.BlockSpec((D_MODEL, TN_ADA), lambda j: (0, j)),
            pl.BlockSpec((1, TN_ADA), lambda j: (0, j)),
        ],
        out_specs=pl.BlockSpec((8, TN_ADA), lambda j: (0, j)),
        out_shape=jax.ShapeDtypeStruct((8, n), F32),
        compiler_params=_params(("arbitrary",)),
        name="adaln_mod",
    )(cvec, w_ada, b_ada.reshape(1, n))


def _ffn_kernel(h_ref, mod_ref, nw_ref, wg_ref, wu_ref, wd_ref, nw2_ref, *rest, mod_base, final):
    if final:
        o_ref, u_scr, acc_scr, vec_scr = rest
    else:
        o_ref, u_ref, u_scr, acc_scr, vec_scr = rest
    j = pl.program_id(1)
    nj = pl.num_programs(1)

    def mod_row(k):
        return mod_ref[0, mod_base + k:mod_base + k + 1, :]

    def unit_rms(x):
        return x * lax.rsqrt(jnp.mean(x * x, axis=-1, keepdims=True) + EPS)

    @pl.when(j == 0)
    def _():
        vec_scr[0:1, :] = nw_ref[...] * (1.0 + mod_row(1))
        vec_scr[1:2, :] = 0.5 * mod_row(2)
        vec_scr[2:3, :] = nw2_ref[...] if final else nw2_ref[...] * (1.0 + mod_row(4))

        def slab(r):
            u_scr[r, :] = (unit_rms(h_ref[r, :]) * vec_scr[0:1, :] + mod_row(0)).astype(BF16)
            acc_scr[r, :] = jnp.zeros((ROW_SLAB, D_MODEL), F32)

        _for_row_slabs(h_ref.shape[0], slab)

    u = u_scr[...]
    a = _silu(_dot(u, wg_ref[...])) * _dot(u, wu_ref[...])
    acc_scr[...] += _dot(a.astype(BF16), wd_ref[...])

    @pl.when(j == nj - 1)
    def _():
        def slab(r):
            hn = h_ref[r, :] + vec_scr[1:2, :] * acc_scr[r, :]
            if final:
                o_ref[r, :] = unit_rms(hn) * vec_scr[2:3, :]
            else:
                o_ref[r, :] = hn
                u_ref[r, :] = (unit_rms(hn) * vec_scr[2:3, :] + mod_row(3)).astype(BF16)

        _for_row_slabs(h_ref.shape[0], slab)


def _ffn_call(h, mod, nw, wg, wu, wd, nw2, *, mod_base, final):
    t = h.shape[0]
    tpm = t // mod.shape[0] // TM
    row = pl.BlockSpec((TM, D_MODEL), lambda i, j: (i, 0))
    vec = pl.BlockSpec((1, D_MODEL), lambda i, j: (0, 0))
    out_shape = [jax.ShapeDtypeStruct((t, D_MODEL), F32)]
    out_specs = [row]
    if not final:
        out_shape.append(jax.ShapeDtypeStruct((t, D_MODEL), BF16))
        out_specs.append(row)
    return pl.pallas_call(
        functools.partial(_ffn_kernel, mod_base=mod_base, final=final),
        grid=(t // TM, D_FF // TF),
        in_specs=[
            row,
            pl.BlockSpec((1, N_MOD, D_MODEL), lambda i, j: (i // tpm, 0, 0)),
            vec,
            pl.BlockSpec((D_MODEL, TF), lambda i, j: (0, j)),
            pl.BlockSpec((D_MODEL, TF), lambda i, j: (0, j)),
            pl.BlockSpec((TF, D_MODEL), lambda i, j: (j, 0)),
            vec,
        ],
        out_specs=out_specs,
        out_shape=out_shape,
        scratch_shapes=[pltpu.VMEM((TM, D_MODEL), BF16), pltpu.VMEM((TM, D_MODEL), F32),
                        pltpu.VMEM((8, D_MODEL), F32)],
        compiler_params=_params(("parallel", "arbitrary")),
        name="ffn_final" if final else "ffn",
    )(h, mod, nw, wg, wu, wd, nw2)


def _convproj_kernel(u_ref, wb_ref, wc_ref, wx_ref, cw_ref, o_ref, *, row_len):
    u = u_ref[...]
    v = _dot_nt(u, wc_ref[...]) * _dot_nt(u, wx_ref[...])
    pos = lax.broadcasted_iota(jnp.int32, v.shape, 0) & (row_len - 1)
    v_prev = jnp.where(pos == 0, 0.0, pltpu.roll(v, 1, 0))
    v_next = jnp.where(pos == row_len - 1, 0.0, pltpu.roll(v, TM - 1, 0))
    cw = cw_ref[...]
    conv = cw[0:1, :] * v_prev + cw[1:2, :] * v + cw[2:3, :] * v_next
    o_ref[...] = (_dot_nt(u, wb_ref[...]) * conv).astype(BF16)


def _w_in_rows(name, tn):
    first = IN_OFFSET[name] // tn
    assert first * tn == IN_OFFSET[name]
    return pl.BlockSpec((tn, D_MODEL), lambda i, j: (first + j, 0))


def _convproj_call(u, w_in_t, conv_w, *, row_len):
    t = u.shape[0]
    return pl.pallas_call(
        functools.partial(_convproj_kernel, row_len=row_len),
        grid=(t // TM, CONV_W // TN),
        in_specs=[
            pl.BlockSpec((TM, D_MODEL), lambda i, j: (i, 0)),
            _w_in_rows("cb", TN), _w_in_rows("cc", TN), _w_in_rows("cx", TN),
            pl.BlockSpec((3, TN), lambda i, j: (0, j)),
        ],
        out_specs=pl.BlockSpec((TM, TN), lambda i, j: (i, j)),
        out_shape=jax.ShapeDtypeStruct((t, CONV_W), BF16),
        compiler_params=_params(("parallel", "arbitrary")),
        name="conv_proj",
    )(u, w_in_t, w_in_t, w_in_t, conv_w)


def _softplus(x):
    return jnp.maximum(x, 0.0) + jnp.log1p(jnp.exp(-jnp.abs(x)))


def _xbcproj_kernel(u_ref, w_ref, cw_ref, cb_ref, o_ref, *, seq_len):
    v = _dot_nt(u_ref[...], w_ref[...])
    rows = v.shape[0]
    pos = lax.broadcasted_iota(jnp.int32, v.shape, 0) & (seq_len - 1)
    v_prev = jnp.where(pos == 0, 0.0, pltpu.roll(v, 1, 0))
    v_next = jnp.where(pos == seq_len - 1, 0.0, pltpu.roll(v, rows - 1, 0))
    cw = cw_ref[...]
    o_ref[...] = _silu(cw[0:1, :] * v_prev + cw[1:2, :] * v + cw[2:3, :] * v_next + cb_ref[...])


def _xbcproj_call(u, w_in_t, conv_w, conv_b, *, seq_len):
    t = u.shape[0]
    tm = max(TM_1, seq_len)
    tn = TN_1 * TM_1 // tm
    assert tm % seq_len == 0 and seq_len & (seq_len - 1) == 0
    return pl.pallas_call(
        functools.partial(_xbcproj_kernel, seq_len=seq_len),
        grid=(t // tm, XBC_W // tn),
        in_specs=[
            pl.BlockSpec((tm, D_MODEL), lambda i, j: (i, 0)),
            _w_in_rows("xbc", tn),
            pl.BlockSpec((3, tn), lambda i, j: (0, j)),
            pl.BlockSpec((1, tn), lambda i, j: (0, j)),
        ],
        out_specs=pl.BlockSpec((tm, tn), lambda i, j: (i, j)),
        out_shape=jax.ShapeDtypeStruct((t, XBC_W), F32),
        compiler_params=_params(("parallel", "arbitrary")),
        name="xbc_proj",
    )(u, w_in_t, conv_w, conv_b)


def _dtproj_kernel(u_ref, w_ref, b_ref, o_ref):
    o_ref[...] = _softplus(_dot_nt(u_ref[...], w_ref[...]) + b_ref[...])


def _dtproj_call(u, w_dt_t, dt_bias):
    t = u.shape[0]
    ndt = w_dt_t.shape[0]
    return pl.pallas_call(
        _dtproj_kernel,
        grid=(t // TM_1,),
        in_specs=[
            pl.BlockSpec((TM_1, D_MODEL), lambda i: (i, 0)),
            pl.BlockSpec((ndt, D_MODEL), lambda i: (0, 0)),
            pl.BlockSpec((1, ndt), lambda i: (0, 0)),
        ],
        out_specs=pl.BlockSpec((TM_1, ndt), lambda i: (i, 0)),
        out_shape=jax.ShapeDtypeStruct((t, ndt), F32),
        compiler_params=_params(("parallel",)),
        name="dt_proj",
    )(u, w_dt_t, dt_bias)


def _split3(a):
    a1 = a.astype(BF16)
    r1 = a - a1.astype(F32)
    a2 = r1.astype(BF16)
    a3 = (r1 - a2.astype(F32)).astype(BF16)
    return a1, a2, a3


def _ssd_kernel(x_ref, b_ref, c_ref, dt_ref, alog_ref, dsk_ref, *rest, n_seq, seq_len, has_h0):
    if has_h0:
        h0f_ref, h0b_ref, y_ref, *scratch = rest
    else:
        y_ref, hf_ref, hb_ref, *scratch = rest
    x_st, b_bf, c_bf, cum_t, dt_t, w_t, dec, st_f, st_b = scratch
    q = CHUNK
    cps = seq_len // q
    nc = n_seq * cps
    hpg = HEADS_PER_GROUP
    nh = 2 * hpg

    lane = lax.broadcasted_iota(jnp.int32, (q, LANES), 1)
    row = lax.broadcasted_iota(jnp.int32, (q, q), 0)
    col = lax.broadcasted_iota(jnp.int32, (q, q), 1)
    tril = row >= col
    triu = row <= col
    lo_half = lane < SSM_HEADDIM
    lo_half_row = lo_half[0:1, :]

    def rows(c):
        return pl.ds(pl.multiple_of(c * q, q), q)

    def pair(p):
        return slice(p * LANES, (p + 1) * LANES)

    def cumsums(c, carry):
        r = rows(c)
        dt = dt_ref[r, :]
        a_row = jnp.where(lane[0:1, :] < nh, -LOG2_E * jnp.exp(alog_ref[...]), 0.0)
        a = dt * a_row
        tril_b = jnp.where(tril, 1.0, 0.0).astype(BF16)
        p3 = _dot(tril_b, jnp.concatenate(_split3(a), axis=1))
        pre = p3[:, 0:LANES] + p3[:, LANES:2 * LANES] + p3[:, 2 * LANES:]
        suf = (pre[q - 1:q, :] - pre) + a
        cm_t = jnp.where(lane < hpg, pre, suf).T[0:nh, :]
        dtt = dt.T[0:nh, :]
        cum_t[c] = cm_t
        dt_t[c] = dtt

        is_fwd = lax.broadcasted_iota(jnp.int32, (nh, q), 0) < hpg
        edge = jnp.where(is_fwd, cm_t[:, q - 1:q], cm_t[:, 0:1])
        w_t[c] = jnp.exp2(edge - cm_t) * dtt
        dec[c] = jnp.exp2(edge)
        return carry

    lax.fori_loop(0, nc, cumsums, 0, unroll=4)

    def own_states(c, carry):
        r = rows(c)
        x = x_ref[r, :]
        bm = b_ref[r, :]
        y_ref[r, :] = dsk_ref[...] * x
        lo_x = (lax.broadcasted_iota(jnp.int32, x.shape, 1) & (LANES - 1)) < SSM_HEADDIM
        x2 = jnp.concatenate([jnp.where(lo_x, x, 0.0), jnp.where(lo_x, 0.0, x)], axis=0).astype(BF16)
        x_st[c] = x2
        b_bf[r, :] = bm.astype(BF16)
        c_bf[r, :] = c_ref[r, :].astype(BF16)
        w = w_t[c]
        b_t = bm.T
        for p in range(PAIRS_PER_GROUP):
            lhs = jnp.concatenate(
                [jnp.concatenate([b_t * w[h:h + 1, :], b_t * w[h + 1:h + 2, :]], axis=1)
                 for h in (2 * p, hpg + 2 * p)], axis=0).astype(BF16)
            own = _dot(lhs, x2[:, pair(p)])
            st_f[c, :, pair(p)] = own[0:SSM_STATE, :]
            st_b[c, :, pair(p)] = own[SSM_STATE:, :]
        return carry

    lax.fori_loop(0, nc, own_states, 0, unroll=2)

    def pair_decay(c, h):
        d = dec[c]
        return jnp.where(lo_half_row, d[h:h + 1, :], d[h + 1:h + 2, :])

    for p in range(PAIRS_PER_GROUP):
        for s in range(n_seq):
            first = s * cps
            if has_h0:
                init = (h0f_ref[s, p].T, h0b_ref[s, p].T)
            else:
                init = (jnp.zeros((SSM_STATE, LANES), F32), jnp.zeros((SSM_STATE, LANES), F32))

            def chain(k, carry, p=p, first=first):
                sf, sb = carry
                cf = first + k
                cb = first + cps - 1 - k
                own_f = st_f[cf, :, pair(p)]
                own_b = st_b[cb, :, pair(p)]
                st_f[cf, :, pair(p)] = sf
                st_b[cb, :, pair(p)] = sb
                return (pair_decay(cf, 2 * p) * sf + own_f, pair_decay(cb, hpg + 2 * p) * sb + own_b)

            sf, sb = lax.fori_loop(0, cps, chain, init)
            if not has_h0:
                hf_ref[s, p] = sf.T
                hb_ref[s, p] = sb.T

    def main(c, carry):
        r = rows(c)
        cm_t = cum_t[c]
        dtt = dt_t[c]
        cb = c_bf[r, :]
        g = _dot_nt(cb, b_bf[r, :])
        off_f = _dot(cb, st_f[c].astype(BF16))
        off_b = _dot(cb, st_b[c].astype(BF16))
        for p in range(PAIRS_PER_GROUP):
            ms, col_f, col_b = [], [], []
            for e in range(2):
                hf = 2 * p + e
                hb = hpg + hf
                ci_f = jnp.broadcast_to(cm_t[hf:hf + 1, :], (q, q)).T
                ci_b = jnp.broadcast_to(cm_t[hb:hb + 1, :], (q, q)).T
                m_f = jnp.exp2(jnp.where(tril, ci_f - cm_t[hf:hf + 1, :], NEG_BIG)) * dtt[hf:hf + 1, :]
                m_b = jnp.exp2(jnp.where(triu, ci_b - cm_t[hb:hb + 1, :], NEG_BIG)) * dtt[hb:hb + 1, :]
                ms.append(((m_f + m_b) * g).astype(BF16))
                col_f.append(ci_f)
                col_b.append(ci_b)
            e_f = jnp.exp2(jnp.where(lo_half, col_f[0], col_f[1]))
            e_b = jnp.exp2(jnp.where(lo_half, col_b[0], col_b[1]))
            y_ref[r, pair(p)] += (_dot(jnp.concatenate(ms, axis=1), x_st[c, :, pair(p)])
                                  + e_f * off_f[:, pair(p)] + e_b * off_b[:, pair(p)])
        return carry

    lax.fori_loop(0, nc, main, 0, unroll=2)


def _ssd_call(xbc, dt, alog, dskip, h0f, h0b, *, seq_len):
    t = xbc.shape[0]
    n_seq = max(1, SSD_ROWS // seq_len)
    blk = n_seq * seq_len
    nc = blk // CHUNK
    has_h0 = h0f is not None
    gx = SSM_INNER // LANES
    gc = gx + SSM_GROUPS
    st_spec = pl.BlockSpec((n_seq, PAIRS_PER_GROUP, LANES, SSM_STATE), lambda s, g: (s, g, 0, 0))
    in_specs = [
        pl.BlockSpec((blk, GROUP_W), lambda s, g: (s, g)),
        pl.BlockSpec((blk, LANES), lambda s, g: (s, gx + g)),
        pl.BlockSpec((blk, LANES), lambda s, g: (s, gc + g)),
        pl.BlockSpec((blk, LANES), lambda s, g: (s, g)),
        pl.BlockSpec((1, LANES), lambda s, g: (0, g)),
        pl.BlockSpec((1, GROUP_W), lambda s, g: (0, g)),
    ]
    args = [xbc, xbc, xbc, dt, alog, dskip]
    y_shape = jax.ShapeDtypeStruct((t, SSM_INNER), F32)
    y_spec = pl.BlockSpec((blk, GROUP_W), lambda s, g: (s, g))
    st_shape = jax.ShapeDtypeStruct((t // seq_len, SSM_HEADS // 2, LANES, SSM_STATE), F32)
    if has_h0:
        in_specs += [st_spec, st_spec]
        args += [h0f, h0b]
        out_specs, out_shape = [y_spec], [y_shape]
    else:
        out_specs, out_shape = [y_spec, st_spec, st_spec], [y_shape, st_shape, st_shape]
    chunk_states = (nc, SSM_STATE, GROUP_W)
    return pl.pallas_call(
        functools.partial(_ssd_kernel, n_seq=n_seq, seq_len=seq_len, has_h0=has_h0),
        grid=(t // blk, SSM_GROUPS),
        in_specs=in_specs,
        out_specs=out_specs,
        out_shape=out_shape,
        scratch_shapes=[
            pltpu.VMEM((nc, 2 * CHUNK, GROUP_W), BF16),
            pltpu.VMEM((blk, LANES), BF16),
            pltpu.VMEM((blk, LANES), BF16),
            pltpu.VMEM((nc, 2 * HEADS_PER_GROUP, CHUNK), F32),
            pltpu.VMEM((nc, 2 * HEADS_PER_GROUP, CHUNK), F32),
            pltpu.VMEM((nc, 2 * HEADS_PER_GROUP, CHUNK), F32),
            pltpu.VMEM((nc, 2 * HEADS_PER_GROUP, CHUNK), F32),
            pltpu.VMEM(chunk_states, F32),
            pltpu.VMEM(chunk_states, F32),
        ],
        compiler_params=_params(("parallel", "arbitrary")),
        name="ssd_latent" if has_h0 else "ssd_context",
    )(*args)


def _gatednorm_kernel(u_ref, y_ref, wz_ref, nw_ref, o_ref, t_scr, ss_scr):
    j = pl.program_id(1)
    nj = SSM_INNER // TN_1
    t = y_ref[...] * _silu(_dot_nt(u_ref[...], wz_ref[...]))
    t_scr[j] = t
    part = jnp.sum(t * t, axis=-1, keepdims=True)

    @pl.when(j == 0)
    def _():
        ss_scr[...] = part

    @pl.when(j > 0)
    def _():
        ss_scr[...] += part

    @pl.when(j == nj - 1)
    def _():
        def slab(r):
            inv = lax.rsqrt(ss_scr[r, :] * (1.0 / SSM_INNER) + EPS)
            for k in range(nj):
                sl = slice(k * TN_1, (k + 1) * TN_1)
                o_ref[r, sl] = (t_scr[k, r, :] * inv * nw_ref[:, sl]).astype(BF16)

        _for_row_slabs(o_ref.shape[0], slab)


def _gatednorm_call(u, y, w_in_t, nw):
    t = u.shape[0]
    nj = SSM_INNER // TN_1
    return pl.pallas_call(
        _gatednorm_kernel,
        grid=(t // TM_1, nj),
        in_specs=[
            pl.BlockSpec((TM_1, D_MODEL), lambda i, j: (i, 0)),
            pl.BlockSpec((TM_1, TN_1), lambda i, j: (i, j)),
            _w_in_rows("z", TN_1),
            pl.BlockSpec((1, SSM_INNER), lambda i, j: (0, 0)),
        ],
        out_specs=pl.BlockSpec((TM_1, SSM_INNER), lambda i, j: (i, 0)),
        out_shape=jax.ShapeDtypeStruct((t, SSM_INNER), BF16),
        scratch_shapes=[pltpu.VMEM((nj, TM_1, TN_1), F32), pltpu.VMEM((TM_1, 1), F32)],
        compiler_params=_params(("parallel", "arbitrary")),
        name="gated_norm",
    )(u, y, w_in_t, nw)


def _merge_kernel(u_ref, cv_ref, yn_ref, wgc_ref, wco_ref, wgs_ref, wso_ref, o_ref):
    u = u_ref[...]
    m = jax.nn.sigmoid(_dot_nt(u, wgc_ref[...])) * _dot(cv_ref[...], wco_ref[...])
    m += jax.nn.sigmoid(_dot_nt(u, wgs_ref[...])) * _dot(yn_ref[...], wso_ref[...])
    o_ref[...] = m.astype(BF16)


def _merge_call(u, cv, yn, wgc_t, wco, wgs_t, wso):
    t = u.shape[0]
    row = pl.BlockSpec((TM, D_MODEL), lambda i, j: (i, 0))
    wspec = pl.BlockSpec((D_MODEL, TN), lambda i, j: (0, j))
    wspec_t = pl.BlockSpec((TN, D_MODEL), lambda i, j: (j, 0))
    return pl.pallas_call(
        _merge_kernel,
        grid=(t // TM, D_MODEL // TN),
        in_specs=[row, row, row, wspec_t, wspec, wspec_t, wspec],
        out_specs=pl.BlockSpec((TM, TN), lambda i, j: (i, j)),
        out_shape=jax.ShapeDtypeStruct((t, D_MODEL), BF16),
        compiler_params=_params(("parallel", "arbitrary")),
        name="merge",
    )(u, cv, yn, wgc_t, wco, wgs_t, wso)


def _oproj_kernel(m_ref, h_ref, mod_ref, wo_ref, o_ref):
    o_ref[...] = h_ref[...] + mod_ref[0, 5:6, :] * _dot(m_ref[...], wo_ref[...])


def _oproj_call(m, h, mod, wo):
    t = m.shape[0]
    tpm = t // mod.shape[0] // TM_1
    tile = pl.BlockSpec((TM_1, TN_1), lambda i, j: (i, j))
    return pl.pallas_call(
        _oproj_kernel,
        grid=(t // TM_1, D_MODEL // TN_1),
        in_specs=[
            pl.BlockSpec((TM_1, D_MODEL), lambda i, j: (i, 0)),
            tile,
            pl.BlockSpec((1, N_MOD, TN_1), lambda i, j: (i // tpm, 0, j)),
            pl.BlockSpec((D_MODEL, TN_1), lambda i, j: (0, j)),
        ],
        out_specs=tile,
        out_shape=jax.ShapeDtypeStruct((t, D_MODEL), F32),
        compiler_params=_params(("parallel", "arbitrary")),
        name="out_proj",
    )(m, h, mod, wo)


def _group_lanes(f, b):
    lead = f.shape[:-1]
    fb = jnp.concatenate([f.reshape(lead + (SSM_GROUPS, HEADS_PER_GROUP)),
                          b.reshape(lead + (SSM_GROUPS, HEADS_PER_GROUP))], axis=-1)
    pad = [(0, 0)] * (fb.ndim - 1) + [(0, LANES - 2 * HEADS_PER_GROUP)]
    return jnp.pad(fb, pad).reshape(lead + (SSM_GROUPS * LANES,))


def _trunk_layer(h, mod, wts, *, seq_len, row_len, h0f, h0b):
    h1, u2 = _ffn_call(h, mod, wts["norm1"], wts["f1g"], wts["f1u"], wts["f1d"], wts["norm2"],
                       mod_base=0, final=False)
    cv = _convproj_call(u2, wts["w_in_t"], wts["conv_w"], row_len=row_len)
    xbc = _xbcproj_call(u2, wts["w_in_t"], wts["ssm_conv_w"], wts["ssm_conv_b"], seq_len=seq_len)
    dt = _dtproj_call(u2, wts["w_dt_t"], wts["dt_bias"])
    ssd = _ssd_call(xbc, dt, wts["a_log"], wts["d_skip"], h0f, h0b, seq_len=seq_len)
    yn = _gatednorm_call(u2, ssd[0], wts["w_in_t"], wts["ssm_norm"])
    merged = _merge_call(u2, cv, yn, wts["w_gc_t"], wts["w_co"], wts["w_gs_t"], wts["w_so"])
    h2 = _oproj_call(merged, h1, mod, wts["w_o"])
    return h2, ssd[1:]


def kernel(x_prompt, x_sample, c, state_ssm_fwd, state_ssm_bwd, c_ctx, w_ada, b_ada, norm1_w, ffn1_w_gate, ffn1_w_up, ffn1_w_down, norm2_w, w_in, conv_w, ssm_conv_w, ssm_conv_b, dt_bias_f, dt_bias_b, a_log_f, a_log_b, d_skip, ssm_norm_w, w_conv_out, w_ssm_out, w_o, norm3_w, ffn2_w_gate, ffn2_w_up, ffn2_w_down, final_norm_w):
    depth = w_ada.shape[0]
    assert depth == 1, "the second ffn call is fused with the final norm, so exactly one layer is supported"
    nb, n_ctx, d = x_prompt.shape
    ndec, n_lat, _ = x_sample.shape
    hp = x_prompt.reshape(nb * n_ctx, d)
    hs = x_sample.reshape(ndec * n_lat, d)
    cvec = jnp.zeros((8, d), F32).at[0].set(c_ctx).at[1:1 + ndec].set(c)
    vec = lambda v: v.reshape(1, -1)
    bf = lambda w: w.astype(BF16)
    new_f, new_b = [], []
    for l in range(depth):
        mod = _mod_call(cvec, w_ada[l], b_ada[l]).reshape(8, N_MOD, d)
        w_in_t = bf(w_in[l].T)
        rows = lambda name: w_in_t[IN_OFFSET[name]:IN_OFFSET[name] + IN_WIDTH[name]]
        wts = dict(
            w_in_t=w_in_t, w_gc_t=rows("gc"), w_gs_t=rows("gs"),
            w_dt_t=_group_lanes(rows("dtf").T, rows("dtb").T).T,
            dt_bias=vec(_group_lanes(dt_bias_f[l], dt_bias_b[l])),
            a_log=vec(_group_lanes(a_log_f[l], a_log_b[l])),
            d_skip=vec(jnp.repeat(d_skip[l], SSM_HEADDIM)),
            norm1=vec(norm1_w[l]), norm2=vec(norm2_w[l]), ssm_norm=vec(ssm_norm_w[l]),
            f1g=bf(ffn1_w_gate[l]), f1u=bf(ffn1_w_up[l]), f1d=bf(ffn1_w_down[l]),
            conv_w=conv_w[l], ssm_conv_w=ssm_conv_w[l], ssm_conv_b=vec(ssm_conv_b[l]),
            w_co=bf(w_conv_out[l]), w_so=bf(w_ssm_out[l]), w_o=bf(w_o[l]),
        )
        f2 = (vec(norm3_w[l]), bf(ffn2_w_gate[l]), bf(ffn2_w_up[l]), bf(ffn2_w_down[l]), vec(final_norm_w))
        pair = lambda s: s[:, l].reshape(ndec, SSM_HEADS // 2, 2 * SSM_HEADDIM, SSM_STATE)

        hp2, (hf, hb) = _trunk_layer(hp, mod[0:1], wts, seq_len=n_ctx, row_len=n_ctx, h0f=None, h0b=None)
        hs2, _ = _trunk_layer(hs, mod[1:1 + ndec], wts, seq_len=n_lat, row_len=GRID_W,
                              h0f=pair(state_ssm_fwd), h0b=pair(state_ssm_bwd))
        new_f.append(hf.reshape(nb, SSM_HEADS, SSM_HEADDIM, SSM_STATE))
        new_b.append(hb.reshape(nb, SSM_HEADS, SSM_HEADDIM, SSM_STATE))
        hp = _ffn_call(hp2, mod[0:1], *f2, mod_base=6, final=True)[0]
        hs = _ffn_call(hs2, mod[1:1 + ndec], *f2, mod_base=6, final=True)[0]
    return (hp.reshape(nb, n_ctx, d), hs.reshape(ndec, n_lat, d),
            jnp.stack(new_f, axis=1), jnp.stack(new_b, axis=1))
```

```python
import functools

import jax
import jax.numpy as jnp
from jax import lax
from jax.experimental import pallas as pl
from jax.experimental.pallas import tpu as pltpu

D_MODEL = 2048
D_FF = 5632
GRID_W = 64
CONV_W = 2048
SSM_INNER = 2048
SSM_HEADDIM = 64
SSM_HEADS = SSM_INNER // SSM_HEADDIM
SSM_GROUPS = 4
SSM_STATE = 128
CHUNK = 128
N_MOD = 9
EPS = 1e-6
XBC_W = SSM_INNER + 2 * SSM_GROUPS * SSM_STATE

IN_SPLITS = (("cb", CONV_W), ("cc", CONV_W), ("cx", CONV_W), ("z", SSM_INNER), ("xbc", XBC_W),
             ("dtf", SSM_HEADS), ("dtb", SSM_HEADS), ("gc", D_MODEL), ("gs", D_MODEL))
IN_OFFSET = {}
for _name, _width in IN_SPLITS:
    IN_OFFSET[_name] = sum(w for _, w in IN_SPLITS[:len(IN_OFFSET)])
IN_WIDTH = dict(IN_SPLITS)

HEADS_PER_GROUP = SSM_HEADS // SSM_GROUPS
PAIRS_PER_GROUP = HEADS_PER_GROUP // 2
GROUP_W = HEADS_PER_GROUP * SSM_HEADDIM
LANES = 128
NEG_BIG = -1e30
LOG2_E = 1.4426950408889634

TM = 512
TN = 512
TF = 512
TM_1 = 1024
TN_1 = 1024
TN_ADA = 1024
ROW_SLAB = 16
SSD_ROWS = 1024
VMEM_LIMIT = 56 * 1024 * 1024

F32 = jnp.float32
BF16 = jnp.bfloat16


def _dot(a, b):
    return jnp.dot(a, b, preferred_element_type=F32)


def _dot_nt(a, b):
    return lax.dot_general(a, b, (((1,), (1,)), ((), ())), preferred_element_type=F32)


def _silu(x):
    return x * jax.nn.sigmoid(x)


def _rms(x, w):
    ms = jnp.mean(x * x, axis=-1, keepdims=True)
    return x * lax.rsqrt(ms + EPS) * w


def _for_row_slabs(n_rows, fn, static=False):
    if static:
        for s in range(n_rows // ROW_SLAB):
            fn(slice(s * ROW_SLAB, (s + 1) * ROW_SLAB))
        return

    def body(s, carry):
        fn(pl.ds(pl.multiple_of(s * ROW_SLAB, ROW_SLAB), ROW_SLAB))
        return carry

    lax.fori_loop(0, n_rows // ROW_SLAB, body, 0, unroll=8)


def _params(sem):
    return pltpu.CompilerParams(dimension_semantics=sem, vmem_limit_bytes=VMEM_LIMIT)


def _mod_kernel(c_ref, w_ref, b_ref, o_ref):
    a = _silu(c_ref[...]).astype(BF16)
    o_ref[...] = _dot(a, w_ref[...].astype(BF16)) + b_ref[...]


def _mod_call(cvec, w_ada, b_ada):
    n = w_ada.shape[1]
    return pl.pallas_call(
        _mod_kernel,
        grid=(n // TN_ADA,),
        in_specs=[
            pl.BlockSpec((8, D_MODEL), lambda j: (0, 0)),
            pl.BlockSpec((D_MODEL, TN_ADA), lambda j: (0, j)),
            pl.BlockSpec((1, TN_ADA), lambda j: (0, j)),
        ],
        out_specs=pl.BlockSpec((8, TN_ADA), lambda j: (0, j)),
        out_shape=jax.ShapeDtypeStruct((8, n), F32),
        compiler_params=_params(("arbitrary",)),
        name="adaln_mod",
    )(cvec, w_ada, b_ada.reshape(1, n))


def _ffn_kernel(h_ref, mod_ref, nw_ref, wg_ref, wu_ref, wd_ref, nw2_ref, *rest, mod_base, final):
    if final:
        m_ref, wo_ref, o_ref, u_scr, acc_scr, vec_scr = rest
    else:
        o_ref, u_ref, u_scr, acc_scr, vec_scr = rest
    j = pl.program_id(1)
    nj = pl.num_programs(1)

    def mod_row(k):
        return mod_ref[0, mod_base + k:mod_base + k + 1, :]

    def unit_rms(x):
        return x * lax.rsqrt(jnp.mean(x * x, axis=-1, keepdims=True) + EPS)

    @pl.when(j == 0)
    def _():
        vec_scr[0:1, :] = nw_ref[...] * (1.0 + mod_row(1))
        vec_scr[1:2, :] = 0.5 * mod_row(2)
        vec_scr[2:3, :] = nw2_ref[...] if final else nw2_ref[...] * (1.0 + mod_row(4))
        if final:
            acc_scr[...] = _dot(m_ref[...], wo_ref[...])

        def slab(r):
            x = h_ref[r, :]
            if final:
                x = x + mod_row(-1) * acc_scr[r, :]
                o_ref[r, :] = x
            u_scr[r, :] = (unit_rms(x) * vec_scr[0:1, :] + mod_row(0)).astype(BF16)
            acc_scr[r, :] = jnp.zeros((ROW_SLAB, D_MODEL), F32)

        _for_row_slabs(h_ref.shape[0], slab)

    u = u_scr[...]
    a = _silu(_dot(u, wg_ref[...])) * _dot(u, wu_ref[...])
    acc_scr[...] += _dot(a.astype(BF16), wd_ref[...].astype(BF16))

    @pl.when(j == nj - 1)
    def _():
        def slab(r):
            if final:
                hn = o_ref[r, :] + vec_scr[1:2, :] * acc_scr[r, :]
                o_ref[r, :] = unit_rms(hn) * vec_scr[2:3, :]
            else:
                hn = h_ref[r, :] + vec_scr[1:2, :] * acc_scr[r, :]
                o_ref[r, :] = hn
                u_ref[r, :] = (unit_rms(hn) * vec_scr[2:3, :] + mod_row(3)).astype(BF16)

        _for_row_slabs(h_ref.shape[0], slab, static=True)


def _ffn_call(h, mod, nw, wg, wu, wd, nw2, merged=None, wo=None, *, mod_base):
    t = h.shape[0]
    final = merged is not None
    tpm = t // mod.shape[0] // TM
    row = pl.BlockSpec((TM, D_MODEL), lambda i, j: (i, 0))
    vec = pl.BlockSpec((1, D_MODEL), lambda i, j: (0, 0))
    in_specs = [
        row,
        pl.BlockSpec((1, N_MOD, D_MODEL), lambda i, j: (i // tpm, 0, 0)),
        vec,
        pl.BlockSpec((D_MODEL, TF), lambda i, j: (0, j)),
        pl.BlockSpec((D_MODEL, TF), lambda i, j: (0, j)),
        pl.BlockSpec((TF, D_MODEL), lambda i, j: (j, 0)),
        vec,
    ]
    args = [h, mod, nw, wg, wu, wd, nw2]
    out_shape = [jax.ShapeDtypeStruct((t, D_MODEL), F32)]
    out_specs = [row]
    if final:
        in_specs += [row, pl.BlockSpec((D_MODEL, D_MODEL), lambda i, j: (0, 0), pipeline_mode=pl.Buffered(1))]
        args += [merged, wo]
    else:
        out_shape.append(jax.ShapeDtypeStruct((t, D_MODEL), BF16))
        out_specs.append(row)
    return pl.pallas_call(
        functools.partial(_ffn_kernel, mod_base=mod_base, final=final),
        grid=(t // TM, D_FF // TF),
        in_specs=in_specs,
        out_specs=out_specs,
        out_shape=out_shape,
        scratch_shapes=[pltpu.VMEM((TM, D_MODEL), BF16), pltpu.VMEM((TM, D_MODEL), F32),
                        pltpu.VMEM((8, D_MODEL), F32)],
        compiler_params=_params(("parallel", "arbitrary")),
        name="ffn_final" if final else "ffn",
    )(*args)


def _convproj_kernel(u_ref, wb_ref, wc_ref, wx_ref, cw_ref, o_ref, *, row_len):
    u = u_ref[...]
    v = _dot_nt(u, wc_ref[...]) * _dot_nt(u, wx_ref[...])
    pos = lax.broadcasted_iota(jnp.int32, v.shape, 0) & (row_len - 1)
    v_prev = jnp.where(pos == 0, 0.0, pltpu.roll(v, 1, 0))
    v_next = jnp.where(pos == row_len - 1, 0.0, pltpu.roll(v, TM - 1, 0))
    cw = cw_ref[...]
    conv = cw[0:1, :] * v_prev + cw[1:2, :] * v + cw[2:3, :] * v_next
    o_ref[...] = (_dot_nt(u, wb_ref[...]) * conv).astype(BF16)


def _w_in_rows(name, tn):
    first = IN_OFFSET[name] // tn
    assert first * tn == IN_OFFSET[name]
    return pl.BlockSpec((tn, D_MODEL), lambda i, j: (first + j, 0))


def _convproj_call(u, w_in_t, conv_w, *, row_len):
    t = u.shape[0]
    return pl.pallas_call(
        functools.partial(_convproj_kernel, row_len=row_len),
        grid=(t // TM, CONV_W // TN),
        in_specs=[
            pl.BlockSpec((TM, D_MODEL), lambda i, j: (i, 0)),
            _w_in_rows("cb", TN), _w_in_rows("cc", TN), _w_in_rows("cx", TN),
            pl.BlockSpec((3, TN), lambda i, j: (0, j)),
        ],
        out_specs=pl.BlockSpec((TM, TN), lambda i, j: (i, j)),
        out_shape=jax.ShapeDtypeStruct((t, CONV_W), BF16),
        compiler_params=_params(("parallel", "arbitrary")),
        name="conv_proj",
    )(u, w_in_t, w_in_t, w_in_t, conv_w)


def _softplus(x):
    return jnp.maximum(x, 0.0) + jnp.log1p(jnp.exp(-jnp.abs(x)))


def _xbcproj_kernel(u_ref, w_ref, cw_ref, cb_ref, o_ref, *, seq_len):
    v = _dot_nt(u_ref[...], w_ref[...])
    rows = v.shape[0]
    pos = lax.broadcasted_iota(jnp.int32, v.shape, 0) & (seq_len - 1)
    v_prev = jnp.where(pos == 0, 0.0, pltpu.roll(v, 1, 0))
    v_next = jnp.where(pos == seq_len - 1, 0.0, pltpu.roll(v, rows - 1, 0))
    cw = cw_ref[...]
    o_ref[...] = _silu(cw[0:1, :] * v_prev + cw[1:2, :] * v + cw[2:3, :] * v_next + cb_ref[...])


def _xbcproj_call(u, w_in_t, conv_w, conv_b, *, seq_len):
    t = u.shape[0]
    tm = max(TM_1, seq_len)
    tn = TN_1 * TM_1 // tm
    assert tm % seq_len == 0 and seq_len & (seq_len - 1) == 0
    return pl.pallas_call(
        functools.partial(_xbcproj_kernel, seq_len=seq_len),
        grid=(t // tm, XBC_W // tn),
        in_specs=[
            pl.BlockSpec((tm, D_MODEL), lambda i, j: (i, 0)),
            _w_in_rows("xbc", tn),
            pl.BlockSpec((3, tn), lambda i, j: (0, j)),
            pl.BlockSpec((1, tn), lambda i, j: (0, j)),
        ],
        out_specs=pl.BlockSpec((tm, tn), lambda i, j: (i, j)),
        out_shape=jax.ShapeDtypeStruct((t, XBC_W), F32),
        compiler_params=_params(("parallel", "arbitrary")),
        name="xbc_proj",
    )(u, w_in_t, conv_w, conv_b)


def _dtproj_kernel(u_ref, w_ref, b_ref, o_ref):
    o_ref[...] = _softplus(_dot_nt(u_ref[...], w_ref[...]) + b_ref[...])


def _dtproj_call(u, w_dt_t, dt_bias):
    t = u.shape[0]
    ndt = w_dt_t.shape[0]
    return pl.pallas_call(
        _dtproj_kernel,
        grid=(t // TM_1,),
        in_specs=[
            pl.BlockSpec((TM_1, D_MODEL), lambda i: (i, 0)),
            pl.BlockSpec((ndt, D_MODEL), lambda i: (0, 0)),
            pl.BlockSpec((1, ndt), lambda i: (0, 0)),
        ],
        out_specs=pl.BlockSpec((TM_1, ndt), lambda i: (i, 0)),
        out_shape=jax.ShapeDtypeStruct((t, ndt), F32),
        compiler_params=_params(("parallel",)),
        name="dt_proj",
    )(u, w_dt_t, dt_bias)


def _split3(a):
    a1 = a.astype(BF16)
    r1 = a - a1.astype(F32)
    a2 = r1.astype(BF16)
    a3 = (r1 - a2.astype(F32)).astype(BF16)
    return a1, a2, a3


def _ssd_kernel(x_ref, b_ref, c_ref, dt_ref, alog_ref, dsk_ref, *rest, n_seq, seq_len, has_h0):
    if has_h0:
        h0f_ref, h0b_ref, y_ref, *scratch = rest
    else:
        y_ref, hf_ref, hb_ref, *scratch = rest
    x_st, b_bf, c_bf, cum_t, dt_t, w_t, dec, st_f, st_b = scratch
    q = CHUNK
    cps = seq_len // q
    nc = n_seq * cps
    hpg = HEADS_PER_GROUP
    nh = 2 * hpg

    lane = lax.broadcasted_iota(jnp.int32, (q, LANES), 1)
    row = lax.broadcasted_iota(jnp.int32, (q, q), 0)
    col = lax.broadcasted_iota(jnp.int32, (q, q), 1)
    tril = row >= col
    triu = row <= col
    lo_half = lane < SSM_HEADDIM
    lo_half_row = lo_half[0:1, :]

    def rows(c):
        return pl.ds(pl.multiple_of(c * q, q), q)

    def pair(p):
        return slice(p * LANES, (p + 1) * LANES)

    def cumsums(c, carry):
        r = rows(c)
        dt = dt_ref[r, :]
        a_row = jnp.where(lane[0:1, :] < nh, -LOG2_E * jnp.exp(alog_ref[...]), 0.0)
        a = dt * a_row
        tril_b = jnp.where(tril, 1.0, 0.0).astype(BF16)
        p3 = _dot(tril_b, jnp.concatenate(_split3(a), axis=1))
        pre = p3[:, 0:LANES] + p3[:, LANES:2 * LANES] + p3[:, 2 * LANES:]
        suf = (pre[q - 1:q, :] - pre) + a
        cm_t = jnp.where(lane < hpg, pre, suf).T[0:nh, :]
        dtt = dt.T[0:nh, :]
        cum_t[c] = cm_t
        dt_t[c] = dtt

        is_fwd = lax.broadcasted_iota(jnp.int32, (nh, q), 0) < hpg
        edge = jnp.where(is_fwd, cm_t[:, q - 1:q], cm_t[:, 0:1])
        w_t[c] = jnp.exp2(edge - cm_t) * dtt
        dec[c] = jnp.exp2(edge)
        return carry

    lax.fori_loop(0, nc, cumsums, 0, unroll=4)

    def own_states(c, carry):
        r = rows(c)
        x = x_ref[r, :]
        bm = b_ref[r, :]
        y_ref[r, :] = dsk_ref[...] * x
        lo_x = (lax.broadcasted_iota(jnp.int32, x.shape, 1) & (LANES - 1)) < SSM_HEADDIM
        x2 = jnp.concatenate([jnp.where(lo_x, x, 0.0), jnp.where(lo_x, 0.0, x)], axis=0).astype(BF16)
        x_st[c] = x2
        b_bf[r, :] = bm.astype(BF16)
        c_bf[r, :] = c_ref[r, :].astype(BF16)
        w = w_t[c]
        b_t = bm.T
        for p in range(PAIRS_PER_GROUP):
            lhs = jnp.concatenate(
                [jnp.concatenate([b_t * w[h:h + 1, :], b_t * w[h + 1:h + 2, :]], axis=1)
                 for h in (2 * p, hpg + 2 * p)], axis=0).astype(BF16)
            own = _dot(lhs, x2[:, pair(p)])
            st_f[c, :, pair(p)] = own[0:SSM_STATE, :]
            st_b[c, :, pair(p)] = own[SSM_STATE:, :]
        return carry

    lax.fori_loop(0, nc, own_states, 0, unroll=2)

    def pair_decay(c, h):
        d = dec[c]
        return jnp.where(lo_half_row, d[h:h + 1, :], d[h + 1:h + 2, :])

    for p in range(PAIRS_PER_GROUP):
        for s in range(n_seq):
            first = s * cps
            if has_h0:
                init = (h0f_ref[s, p].T, h0b_ref[s, p].T)
            else:
                init = (jnp.zeros((SSM_STATE, LANES), F32), jnp.zeros((SSM_STATE, LANES), F32))

            def chain(k, carry, p=p, first=first):
                sf, sb = carry
                cf = first + k
                cb = first + cps - 1 - k
                own_f = st_f[cf, :, pair(p)]
                own_b = st_b[cb, :, pair(p)]
                st_f[cf, :, pair(p)] = sf
                st_b[cb, :, pair(p)] = sb
                return (pair_decay(cf, 2 * p) * sf + own_f, pair_decay(cb, hpg + 2 * p) * sb + own_b)

            sf, sb = lax.fori_loop(0, cps, chain, init)
            if not has_h0:
                hf_ref[s, p] = sf.T
                hb_ref[s, p] = sb.T

    def main(c, carry):
        r = rows(c)
        cm_t = cum_t[c]
        dtt = dt_t[c]
        cb = c_bf[r, :]
        g = _dot_nt(cb, b_bf[r, :])
        off_f = _dot(cb, st_f[c].astype(BF16))
        off_b = _dot(cb, st_b[c].astype(BF16))
        for p in range(PAIRS_PER_GROUP):
            ms, col_f, col_b = [], [], []
            for e in range(2):
                hf = 2 * p + e
                hb = hpg + hf
                ci_f = jnp.broadcast_to(cm_t[hf:hf + 1, :], (q, q)).T
                ci_b = jnp.broadcast_to(cm_t[hb:hb + 1, :], (q, q)).T
                m_f = jnp.exp2(jnp.where(tril, ci_f - cm_t[hf:hf + 1, :], NEG_BIG)) * dtt[hf:hf + 1, :]
                m_b = jnp.exp2(jnp.where(triu, ci_b - cm_t[hb:hb + 1, :], NEG_BIG)) * dtt[hb:hb + 1, :]
                ms.append(((m_f + m_b) * g).astype(BF16))
                col_f.append(ci_f)
                col_b.append(ci_b)
            e_f = jnp.exp2(jnp.where(lo_half, col_f[0], col_f[1]))
            e_b = jnp.exp2(jnp.where(lo_half, col_b[0], col_b[1]))
            y_ref[r, pair(p)] += (_dot(jnp.concatenate(ms, axis=1), x_st[c, :, pair(p)])
                                  + e_f * off_f[:, pair(p)] + e_b * off_b[:, pair(p)])
        return carry

    lax.fori_loop(0, nc, main, 0, unroll=2)


def _ssd_call(xbc, dt, alog, dskip, h0f, h0b, *, seq_len):
    t = xbc.shape[0]
    n_seq = max(1, SSD_ROWS // seq_len)
    blk = n_seq * seq_len
    nc = blk // CHUNK
    has_h0 = h0f is not None
    gx = SSM_INNER // LANES
    gc = gx + SSM_GROUPS
    st_spec = pl.BlockSpec((n_seq, PAIRS_PER_GROUP, LANES, SSM_STATE), lambda s, g: (s, g, 0, 0))
    in_specs = [
        pl.BlockSpec((blk, GROUP_W), lambda s, g: (s, g)),
        pl.BlockSpec((blk, LANES), lambda s, g: (s, gx + g)),
        pl.BlockSpec((blk, LANES), lambda s, g: (s, gc + g)),
        pl.BlockSpec((blk, LANES), lambda s, g: (s, g)),
        pl.BlockSpec((1, LANES), lambda s, g: (0, g)),
        pl.BlockSpec((1, GROUP_W), lambda s, g: (0, g)),
    ]
    args = [xbc, xbc, xbc, dt, alog, dskip]
    y_shape = jax.ShapeDtypeStruct((t, SSM_INNER), F32)
    y_spec = pl.BlockSpec((blk, GROUP_W), lambda s, g: (s, g))
    st_shape = jax.ShapeDtypeStruct((t // seq_len, SSM_HEADS // 2, LANES, SSM_STATE), F32)
    if has_h0:
        in_specs += [st_spec, st_spec]
        args += [h0f, h0b]
        out_specs, out_shape = [y_spec], [y_shape]
    else:
        out_specs, out_shape = [y_spec, st_spec, st_spec], [y_shape, st_shape, st_shape]
    chunk_states = (nc, SSM_STATE, GROUP_W)
    return pl.pallas_call(
        functools.partial(_ssd_kernel, n_seq=n_seq, seq_len=seq_len, has_h0=has_h0),
        grid=(t // blk, SSM_GROUPS),
        in_specs=in_specs,
        out_specs=out_specs,
        out_shape=out_shape,
        scratch_shapes=[
            pltpu.VMEM((nc, 2 * CHUNK, GROUP_W), BF16),
            pltpu.VMEM((blk, LANES), BF16),
            pltpu.VMEM((blk, LANES), BF16),
            pltpu.VMEM((nc, 2 * HEADS_PER_GROUP, CHUNK), F32),
            pltpu.VMEM((nc, 2 * HEADS_PER_GROUP, CHUNK), F32),
            pltpu.VMEM((nc, 2 * HEADS_PER_GROUP, CHUNK), F32),
            pltpu.VMEM((nc, 2 * HEADS_PER_GROUP, CHUNK), F32),
            pltpu.VMEM(chunk_states, F32),
            pltpu.VMEM(chunk_states, F32),
        ],
        compiler_params=_params(("parallel", "arbitrary")),
        name="ssd_latent" if has_h0 else "ssd_context",
    )(*args)


def _gatednorm_kernel(u_ref, y_ref, wz_ref, nw_ref, o_ref, t_scr, ss_scr):
    j = pl.program_id(1)
    nj = SSM_INNER // TN_1
    t = y_ref[...] * _silu(_dot_nt(u_ref[...], wz_ref[...]))
    t_scr[j] = t
    part = jnp.sum(t * t, axis=-1, keepdims=True)

    @pl.when(j == 0)
    def _():
        ss_scr[...] = part

    @pl.when(j > 0)
    def _():
        ss_scr[...] += part

    @pl.when(j == nj - 1)
    def _():
        def slab(r):
            inv = lax.rsqrt(ss_scr[r, :] * (1.0 / SSM_INNER) + EPS)
            for k in range(nj):
                sl = slice(k * TN_1, (k + 1) * TN_1)
                o_ref[r, sl] = (t_scr[k, r, :] * inv * nw_ref[:, sl]).astype(BF16)

        _for_row_slabs(o_ref.shape[0], slab)


def _gatednorm_call(u, y, w_in_t, nw):
    t = u.shape[0]
    nj = SSM_INNER // TN_1
    return pl.pallas_call(
        _gatednorm_kernel,
        grid=(t // TM_1, nj),
        in_specs=[
            pl.BlockSpec((TM_1, D_MODEL), lambda i, j: (i, 0)),
            pl.BlockSpec((TM_1, TN_1), lambda i, j: (i, j)),
            _w_in_rows("z", TN_1),
            pl.BlockSpec((1, SSM_INNER), lambda i, j: (0, 0)),
        ],
        out_specs=pl.BlockSpec((TM_1, SSM_INNER), lambda i, j: (i, 0)),
        out_shape=jax.ShapeDtypeStruct((t, SSM_INNER), BF16),
        scratch_shapes=[pltpu.VMEM((nj, TM_1, TN_1), F32), pltpu.VMEM((TM_1, 1), F32)],
        compiler_params=_params(("parallel", "arbitrary")),
        name="gated_norm",
    )(u, y, w_in_t, nw)


def _merge_kernel(u_ref, cv_ref, yn_ref, wgc_ref, wco_ref, wgs_ref, wso_ref, o_ref):
    u = u_ref[...]
    m = jax.nn.sigmoid(_dot_nt(u, wgc_ref[...])) * _dot(cv_ref[...], wco_ref[...])
    m += jax.nn.sigmoid(_dot_nt(u, wgs_ref[...])) * _dot(yn_ref[...], wso_ref[...])
    o_ref[...] = m.astype(BF16)


def _merge_call(u, cv, yn, wgc_t, wco, wgs_t, wso):
    t = u.shape[0]
    row = pl.BlockSpec((TM, D_MODEL), lambda i, j: (i, 0))
    wspec = pl.BlockSpec((D_MODEL, TN), lambda i, j: (0, j))
    wspec_t = pl.BlockSpec((TN, D_MODEL), lambda i, j: (j, 0))
    return pl.pallas_call(
        _merge_kernel,
        grid=(t // TM, D_MODEL // TN),
        in_specs=[row, row, row, wspec_t, wspec, wspec_t, wspec],
        out_specs=pl.BlockSpec((TM, TN), lambda i, j: (i, j)),
        out_shape=jax.ShapeDtypeStruct((t, D_MODEL), BF16),
        compiler_params=_params(("parallel", "arbitrary")),
        name="merge",
    )(u, cv, yn, wgc_t, wco, wgs_t, wso)


def _group_lanes(f, b):
    lead = f.shape[:-1]
    fb = jnp.concatenate([f.reshape(lead + (SSM_GROUPS, HEADS_PER_GROUP)),
                          b.reshape(lead + (SSM_GROUPS, HEADS_PER_GROUP))], axis=-1)
    pad = [(0, 0)] * (fb.ndim - 1) + [(0, LANES - 2 * HEADS_PER_GROUP)]
    return jnp.pad(fb, pad).reshape(lead + (SSM_GROUPS * LANES,))


def _trunk_layer(h, mod, wts, *, seq_len, row_len, h0f, h0b):
    h1, u2 = _ffn_call(h, mod, wts["norm1"], wts["f1g"], wts["f1u"], wts["f1d"], wts["norm2"], mod_base=0)
    cv = _convproj_call(u2, wts["w_in_t"], wts["conv_w"], row_len=row_len)
    xbc = _xbcproj_call(u2, wts["w_in_t"], wts["ssm_conv_w"], wts["ssm_conv_b"], seq_len=seq_len)
    dt = _dtproj_call(u2, wts["w_dt_t"], wts["dt_bias"])
    ssd = _ssd_call(xbc, dt, wts["a_log"], wts["d_skip"], h0f, h0b, seq_len=seq_len)
    yn = _gatednorm_call(u2, ssd[0], wts["w_in_t"], wts["ssm_norm"])
    merged = _merge_call(u2, cv, yn, wts["w_gc_t"], wts["w_co"], wts["w_gs_t"], wts["w_so"])
    out = _ffn_call(h1, mod, wts["norm3"], wts["f2g"], wts["f2u"], wts["f2d"], wts["final_norm"],
                    merged, wts["w_o"], mod_base=6)[0]
    return out, ssd[1:]


def kernel(x_prompt, x_sample, c, state_ssm_fwd, state_ssm_bwd, c_ctx, w_ada, b_ada, norm1_w, ffn1_w_gate, ffn1_w_up, ffn1_w_down, norm2_w, w_in, conv_w, ssm_conv_w, ssm_conv_b, dt_bias_f, dt_bias_b, a_log_f, a_log_b, d_skip, ssm_norm_w, w_conv_out, w_ssm_out, w_o, norm3_w, ffn2_w_gate, ffn2_w_up, ffn2_w_down, final_norm_w):
    depth = w_ada.shape[0]
    assert depth == 1, "the second ffn call is fused with the final norm, so exactly one layer is supported"
    nb, n_ctx, d = x_prompt.shape
    ndec, n_lat, _ = x_sample.shape
    hp = x_prompt.reshape(nb * n_ctx, d)
    hs = x_sample.reshape(ndec * n_lat, d)
    cvec = jnp.zeros((8, d), F32).at[0].set(c_ctx).at[1:1 + ndec].set(c)
    vec = lambda v: v.reshape(1, -1)
    bf = lambda w: w.astype(BF16)
    new_f, new_b = [], []
    for l in range(depth):
        mod = _mod_call(cvec, w_ada[l], b_ada[l]).reshape(8, N_MOD, d)
        w_in_t = bf(w_in[l].T)
        rows = lambda name: w_in_t[IN_OFFSET[name]:IN_OFFSET[name] + IN_WIDTH[name]]
        wts = dict(
            w_in_t=w_in_t, w_gc_t=rows("gc"), w_gs_t=rows("gs"),
            w_dt_t=_group_lanes(rows("dtf").T, rows("dtb").T).T,
            dt_bias=vec(_group_lanes(dt_bias_f[l], dt_bias_b[l])),
            a_log=vec(_group_lanes(a_log_f[l], a_log_b[l])),
            d_skip=vec(jnp.repeat(d_skip[l], SSM_HEADDIM)),
            norm1=vec(norm1_w[l]), norm2=vec(norm2_w[l]), norm3=vec(norm3_w[l]), ssm_norm=vec(ssm_norm_w[l]),
            final_norm=vec(final_norm_w),
            f1g=bf(ffn1_w_gate[l]), f1u=bf(ffn1_w_up[l]), f1d=ffn1_w_down[l],
            f2g=bf(ffn2_w_gate[l]), f2u=bf(ffn2_w_up[l]), f2d=bf(ffn2_w_down[l]),
            conv_w=conv_w[l], ssm_conv_w=ssm_conv_w[l], ssm_conv_b=vec(ssm_conv_b[l]),
            w_co=bf(w_conv_out[l]), w_so=bf(w_ssm_out[l]), w_o=bf(w_o[l]),
        )
        pair = lambda s: s[:, l].reshape(ndec, SSM_HEADS // 2, 2 * SSM_HEADDIM, SSM_STATE)

        hp, (hf, hb) = _trunk_layer(hp, mod[0:1], wts, seq_len=n_ctx, row_len=n_ctx, h0f=None, h0b=None)
        hs, _ = _trunk_layer(hs, mod[1:1 + ndec], wts, seq_len=n_lat, row_len=GRID_W,
                             h0f=pair(state_ssm_fwd), h0b=pair(state_ssm_bwd))
        new_f.append(hf.reshape(nb, SSM_HEADS, SSM_HEADDIM, SSM_STATE))
        new_b.append(hb.reshape(nb, SSM_HEADS, SSM_HEADDIM, SSM_STATE))
    return (hp.reshape(nb, n_ctx, d), hs.reshape(ndec, n_lat, d),
            jnp.stack(new_f, axis=1), jnp.stack(new_b, axis=1))
```

```python
import functools

import jax
import jax.numpy as jnp
from jax import lax
from jax.experimental import pallas as pl
from jax.experimental.pallas import tpu as pltpu

D_MODEL = 2048
D_FF = 5632
GRID_W = 64
CONV_W = 2048
SSM_INNER = 2048
SSM_HEADDIM = 64
SSM_HEADS = SSM_INNER // SSM_HEADDIM
SSM_GROUPS = 4
SSM_STATE = 128
CHUNK = 128
N_MOD = 9
EPS = 1e-6
XBC_W = SSM_INNER + 2 * SSM_GROUPS * SSM_STATE

IN_SPLITS = (("cb", CONV_W), ("cc", CONV_W), ("cx", CONV_W), ("z", SSM_INNER), ("xbc", XBC_W),
             ("dtf", SSM_HEADS), ("dtb", SSM_HEADS), ("gc", D_MODEL), ("gs", D_MODEL))
IN_OFFSET = {}
for _name, _width in IN_SPLITS:
    IN_OFFSET[_name] = sum(w for _, w in IN_SPLITS[:len(IN_OFFSET)])
IN_WIDTH = dict(IN_SPLITS)

HEADS_PER_GROUP = SSM_HEADS // SSM_GROUPS
PAIRS_PER_GROUP = HEADS_PER_GROUP // 2
GROUP_W = HEADS_PER_GROUP * SSM_HEADDIM
LANES = 128
NEG_BIG = -1e30
LOG2_E = 1.4426950408889634

TM = 512
TN = 1024
TF = 512
TM_1 = 1024
TN_1 = 1024
TN_ADA = 1024
ROW_SLAB = 16
SSD_ROWS = 1024
VMEM_LIMIT = 56 * 1024 * 1024

F32 = jnp.float32
BF16 = jnp.bfloat16


def _dot(a, b):
    return jnp.dot(a, b, preferred_element_type=F32)


def _dot_nt(a, b):
    return lax.dot_general(a, b, (((1,), (1,)), ((), ())), preferred_element_type=F32)


def _silu(x):
    return x * jax.nn.sigmoid(x)


def _rms(x, w):
    ms = jnp.mean(x * x, axis=-1, keepdims=True)
    return x * lax.rsqrt(ms + EPS) * w


def _for_row_slabs(n_rows, fn, static=False):
    if static:
        for s in range(n_rows // ROW_SLAB):
            fn(slice(s * ROW_SLAB, (s + 1) * ROW_SLAB))
        return

    def body(s, carry):
        fn(pl.ds(pl.multiple_of(s * ROW_SLAB, ROW_SLAB), ROW_SLAB))
        return carry

    lax.fori_loop(0, n_rows // ROW_SLAB, body, 0, unroll=8)


def _params(sem):
    return pltpu.CompilerParams(dimension_semantics=sem, vmem_limit_bytes=VMEM_LIMIT)


def _mod_kernel(c_ref, w_ref, b_ref, o_ref):
    a = _silu(c_ref[...]).astype(BF16)
    o_ref[...] = _dot(a, w_ref[...].astype(BF16)) + b_ref[...]


def _mod_call(cvec, w_ada, b_ada):
    n = w_ada.shape[1]
    return pl.pallas_call(
        _mod_kernel,
        grid=(n // TN_ADA,),
        in_specs=[
            pl.BlockSpec((8, D_MODEL), lambda j: (0, 0)),
            pl.BlockSpec((D_MODEL, TN_ADA), lambda j: (0, j)),
            pl.BlockSpec((1, TN_ADA), lambda j: (0, j)),
        ],
        out_specs=pl.BlockSpec((8, TN_ADA), lambda j: (0, j)),
        out_shape=jax.ShapeDtypeStruct((8, n), F32),
        compiler_params=_params(("arbitrary",)),
        name="adaln_mod",
    )(cvec, w_ada, b_ada.reshape(1, n))


def _ffn_kernel(h_ref, mod_ref, nw_ref, wg_ref, wu_ref, wd_ref, nw2_ref, *rest, mod_base, final):
    if final:
        m_ref, wo_ref, o_ref, u_scr, acc_scr, vec_scr = rest
    else:
        o_ref, u_ref, u_scr, acc_scr, vec_scr = rest
    j = pl.program_id(1)
    nj = pl.num_programs(1)

    def mod_row(k):
        return mod_ref[0, mod_base + k:mod_base + k + 1, :]

    def unit_rms(x):
        return x * lax.rsqrt(jnp.mean(x * x, axis=-1, keepdims=True) + EPS)

    @pl.when(j == 0)
    def _():
        vec_scr[0:1, :] = nw_ref[...] * (1.0 + mod_row(1))
        vec_scr[1:2, :] = 0.5 * mod_row(2)
        vec_scr[2:3, :] = nw2_ref[...] if final else nw2_ref[...] * (1.0 + mod_row(4))
        if final:
            acc_scr[...] = _dot(m_ref[...], wo_ref[...])

        def slab(r):
            x = h_ref[r, :]
            if final:
                x = x + mod_row(-1) * acc_scr[r, :]
                o_ref[r, :] = x
            u_scr[r, :] = (unit_rms(x) * vec_scr[0:1, :] + mod_row(0)).astype(BF16)
            acc_scr[r, :] = jnp.zeros((ROW_SLAB, D_MODEL), F32)

        _for_row_slabs(h_ref.shape[0], slab)

    u = u_scr[...]
    a = _silu(_dot(u, wg_ref[...])) * _dot(u, wu_ref[...])
    acc_scr[...] += _dot(a.astype(BF16), wd_ref[...])

    @pl.when(j == nj - 1)
    def _():
        def slab(r):
            if final:
                hn = o_ref[r, :] + vec_scr[1:2, :] * acc_scr[r, :]
                o_ref[r, :] = unit_rms(hn) * vec_scr[2:3, :]
            else:
                hn = h_ref[r, :] + vec_scr[1:2, :] * acc_scr[r, :]
                o_ref[r, :] = hn
                u_ref[r, :] = (unit_rms(hn) * vec_scr[2:3, :] + mod_row(3)).astype(BF16)

        _for_row_slabs(h_ref.shape[0], slab, static=True)


def _ffn_call(h, mod, nw, wg, wu, wd, nw2, merged=None, wo=None, *, mod_base):
    t = h.shape[0]
    final = merged is not None
    tpm = t // mod.shape[0] // TM
    row = pl.BlockSpec((TM, D_MODEL), lambda i, j: (i, 0))
    vec = pl.BlockSpec((1, D_MODEL), lambda i, j: (0, 0))
    in_specs = [
        row,
        pl.BlockSpec((1, N_MOD, D_MODEL), lambda i, j: (i // tpm, 0, 0)),
        vec,
        pl.BlockSpec((D_MODEL, TF), lambda i, j: (0, j)),
        pl.BlockSpec((D_MODEL, TF), lambda i, j: (0, j)),
        pl.BlockSpec((TF, D_MODEL), lambda i, j: (j, 0)),
        vec,
    ]
    args = [h, mod, nw, wg, wu, wd, nw2]
    out_shape = [jax.ShapeDtypeStruct((t, D_MODEL), F32)]
    out_specs = [row]
    if final:
        in_specs += [row, pl.BlockSpec((D_MODEL, D_MODEL), lambda i, j: (0, 0), pipeline_mode=pl.Buffered(1))]
        args += [merged, wo]
    else:
        out_shape.append(jax.ShapeDtypeStruct((t, D_MODEL), BF16))
        out_specs.append(row)
    return pl.pallas_call(
        functools.partial(_ffn_kernel, mod_base=mod_base, final=final),
        grid=(t // TM, D_FF // TF),
        in_specs=in_specs,
        out_specs=out_specs,
        out_shape=out_shape,
        scratch_shapes=[pltpu.VMEM((TM, D_MODEL), BF16), pltpu.VMEM((TM, D_MODEL), F32),
                        pltpu.VMEM((8, D_MODEL), F32)],
        compiler_params=_params(("parallel", "arbitrary")),
        name="ffn_final" if final else "ffn",
    )(*args)


def _softplus(x):
    return jnp.maximum(x, 0.0) + jnp.log1p(jnp.exp(-jnp.abs(x)))


def _convproj_kernel(u_ref, wb_ref, wc_ref, wx_ref, cw_ref, wdt_ref, dtb_ref, o_ref, dt_ref, *, row_len):
    u = u_ref[...]

    @pl.when(pl.program_id(1) == 0)
    def _():
        dt_ref[...] = _softplus(_dot_nt(u, wdt_ref[...]) + dtb_ref[...])

    v = _dot_nt(u, wc_ref[...]) * _dot_nt(u, wx_ref[...])
    pos = lax.broadcasted_iota(jnp.int32, v.shape, 0) & (row_len - 1)
    v_prev = jnp.where(pos == 0, 0.0, pltpu.roll(v, 1, 0))
    v_next = jnp.where(pos == row_len - 1, 0.0, pltpu.roll(v, TM - 1, 0))
    cw = cw_ref[...]
    conv = cw[0:1, :] * v_prev + cw[1:2, :] * v + cw[2:3, :] * v_next
    o_ref[...] = (_dot_nt(u, wb_ref[...]) * conv).astype(BF16)


def _w_in_rows(name, tn):
    first = IN_OFFSET[name] // tn
    assert first * tn == IN_OFFSET[name]
    return pl.BlockSpec((tn, D_MODEL), lambda i, j: (first + j, 0))


def _convproj_call(u, w_in_t, conv_w, w_dt_t, dt_bias, *, row_len):
    t = u.shape[0]
    ndt = w_dt_t.shape[0]
    return pl.pallas_call(
        functools.partial(_convproj_kernel, row_len=row_len),
        grid=(t // TM, CONV_W // TN),
        in_specs=[
            pl.BlockSpec((TM, D_MODEL), lambda i, j: (i, 0)),
            _w_in_rows("cb", TN), _w_in_rows("cc", TN), _w_in_rows("cx", TN),
            pl.BlockSpec((3, TN), lambda i, j: (0, j)),
            pl.BlockSpec((ndt, D_MODEL), lambda i, j: (0, 0)),
            pl.BlockSpec((1, ndt), lambda i, j: (0, 0)),
        ],
        out_specs=[
            pl.BlockSpec((TM, TN), lambda i, j: (i, j)),
            pl.BlockSpec((TM, ndt), lambda i, j: (i, 0)),
        ],
        out_shape=[
            jax.ShapeDtypeStruct((t, CONV_W), BF16),
            jax.ShapeDtypeStruct((t, ndt), F32),
        ],
        compiler_params=_params(("parallel", "arbitrary")),
        name="conv_proj",
    )(u, w_in_t, w_in_t, w_in_t, conv_w, w_dt_t, dt_bias)


def _xbcproj_kernel(u_ref, w_ref, cw_ref, cb_ref, o_ref, *, seq_len):
    v = _dot_nt(u_ref[...], w_ref[...])
    rows = v.shape[0]
    pos = lax.broadcasted_iota(jnp.int32, v.shape, 0) & (seq_len - 1)
    v_prev = jnp.where(pos == 0, 0.0, pltpu.roll(v, 1, 0))
    v_next = jnp.where(pos == seq_len - 1, 0.0, pltpu.roll(v, rows - 1, 0))
    cw = cw_ref[...]
    o_ref[...] = _silu(cw[0:1, :] * v_prev + cw[1:2, :] * v + cw[2:3, :] * v_next + cb_ref[...])


def _xbcproj_call(u, w_in_t, conv_w, conv_b, *, seq_len):
    t = u.shape[0]
    tm = max(TM_1, seq_len)
    tn = TN_1 * TM_1 // tm
    assert tm % seq_len == 0 and seq_len & (seq_len - 1) == 0
    return pl.pallas_call(
        functools.partial(_xbcproj_kernel, seq_len=seq_len),
        grid=(t // tm, XBC_W // tn),
        in_specs=[
            pl.BlockSpec((tm, D_MODEL), lambda i, j: (i, 0)),
            _w_in_rows("xbc", tn),
            pl.BlockSpec((3, tn), lambda i, j: (0, j)),
            pl.BlockSpec((1, tn), lambda i, j: (0, j)),
        ],
        out_specs=pl.BlockSpec((tm, tn), lambda i, j: (i, j)),
        out_shape=jax.ShapeDtypeStruct((t, XBC_W), F32),
        compiler_params=_params(("parallel", "arbitrary")),
        name="xbc_proj",
    )(u, w_in_t, conv_w, conv_b)


def _split3(a):
    a1 = a.astype(BF16)
    r1 = a - a1.astype(F32)
    a2 = r1.astype(BF16)
    a3 = (r1 - a2.astype(F32)).astype(BF16)
    return a1, a2, a3


def _ssd_kernel(x_ref, b_ref, c_ref, dt_ref, alog_ref, dsk_ref, *rest, n_seq, seq_len, has_h0):
    if has_h0:
        h0f_ref, h0b_ref, y_ref, *scratch = rest
    else:
        y_ref, hf_ref, hb_ref, *scratch = rest
    x_st, b_bf, c_bf, cum_t, dt_t, w_t, dec, st_f, st_b = scratch
    q = CHUNK
    cps = seq_len // q
    nc = n_seq * cps
    hpg = HEADS_PER_GROUP
    nh = 2 * hpg

    lane = lax.broadcasted_iota(jnp.int32, (q, LANES), 1)
    row = lax.broadcasted_iota(jnp.int32, (q, q), 0)
    col = lax.broadcasted_iota(jnp.int32, (q, q), 1)
    tril = row >= col
    triu = row <= col
    lo_half = lane < SSM_HEADDIM
    lo_half_row = lo_half[0:1, :]

    def rows(c):
        return pl.ds(pl.multiple_of(c * q, q), q)

    def pair(p):
        return slice(p * LANES, (p + 1) * LANES)

    def cumsums(c, carry):
        r = rows(c)
        dt = dt_ref[r, :]
        a_row = jnp.where(lane[0:1, :] < nh, -LOG2_E * jnp.exp(alog_ref[...]), 0.0)
        a = dt * a_row
        tril_b = jnp.where(tril, 1.0, 0.0).astype(BF16)
        p3 = _dot(tril_b, jnp.concatenate(_split3(a), axis=1))
        pre = p3[:, 0:LANES] + p3[:, LANES:2 * LANES] + p3[:, 2 * LANES:]
        suf = (pre[q - 1:q, :] - pre) + a
        cm_t = jnp.where(lane < hpg, pre, suf).T[0:nh, :]
        dtt = dt.T[0:nh, :]
        cum_t[c] = cm_t
        dt_t[c] = dtt

        is_fwd = lax.broadcasted_iota(jnp.int32, (nh, q), 0) < hpg
        edge = jnp.where(is_fwd, cm_t[:, q - 1:q], cm_t[:, 0:1])
        w_t[c] = jnp.exp2(edge - cm_t) * dtt
        dec[c] = jnp.exp2(edge)
        return carry

    lax.fori_loop(0, nc, cumsums, 0, unroll=4)

    def own_states(c, carry):
        r = rows(c)
        x = x_ref[r, :]
        bm = b_ref[r, :]
        y_ref[r, :] = dsk_ref[...] * x
        lo_x = (lax.broadcasted_iota(jnp.int32, x.shape, 1) & (LANES - 1)) < SSM_HEADDIM
        x2 = jnp.concatenate([jnp.where(lo_x, x, 0.0), jnp.where(lo_x, 0.0, x)], axis=0).astype(BF16)
        x_st[c] = x2
        b_bf[r, :] = bm.astype(BF16)
        c_bf[r, :] = c_ref[r, :].astype(BF16)
        w = w_t[c]
        b_t = bm.T
        for p in range(PAIRS_PER_GROUP):
            lhs = jnp.concatenate(
                [jnp.concatenate([b_t * w[h:h + 1, :], b_t * w[h + 1:h + 2, :]], axis=1)
                 for h in (2 * p, hpg + 2 * p)], axis=0).astype(BF16)
            own = _dot(lhs, x2[:, pair(p)])
            st_f[c, :, pair(p)] = own[0:SSM_STATE, :]
            st_b[c, :, pair(p)] = own[SSM_STATE:, :]
        return carry

    lax.fori_loop(0, nc, own_states, 0, unroll=2)

    def pair_decay(c, h):
        d = dec[c]
        return jnp.where(lo_half_row, d[h:h + 1, :], d[h + 1:h + 2, :])

    for p in range(PAIRS_PER_GROUP):
        for s in range(n_seq):
            first = s * cps
            if has_h0:
                init = (h0f_ref[s, p].T, h0b_ref[s, p].T)
            else:
                init = (jnp.zeros((SSM_STATE, LANES), F32), jnp.zeros((SSM_STATE, LANES), F32))

            def chain(k, carry, p=p, first=first):
                sf, sb = carry
                cf = first + k
                cb = first + cps - 1 - k
                own_f = st_f[cf, :, pair(p)]
                own_b = st_b[cb, :, pair(p)]
                st_f[cf, :, pair(p)] = sf
                st_b[cb, :, pair(p)] = sb
                return (pair_decay(cf, 2 * p) * sf + own_f, pair_decay(cb, hpg + 2 * p) * sb + own_b)

            sf, sb = lax.fori_loop(0, cps, chain, init)
            if not has_h0:
                hf_ref[s, p] = sf.T
                hb_ref[s, p] = sb.T

    def main(c, carry):
        r = rows(c)
        cm_t = cum_t[c]
        dtt = dt_t[c]
        cb = c_bf[r, :]
        g = _dot_nt(cb, b_bf[r, :])
        off_f = _dot(cb, st_f[c].astype(BF16))
        off_b = _dot(cb, st_b[c].astype(BF16))
        for p in range(PAIRS_PER_GROUP):
            ms, col_f, col_b = [], [], []
            for e in range(2):
                hf = 2 * p + e
                hb = hpg + hf
                ci_f = jnp.broadcast_to(cm_t[hf:hf + 1, :], (q, q)).T
                ci_b = jnp.broadcast_to(cm_t[hb:hb + 1, :], (q, q)).T
                m_f = jnp.exp2(jnp.where(tril, ci_f - cm_t[hf:hf + 1, :], NEG_BIG)) * dtt[hf:hf + 1, :]
                m_b = jnp.exp2(jnp.where(triu, ci_b - cm_t[hb:hb + 1, :], NEG_BIG)) * dtt[hb:hb + 1, :]
                ms.append(((m_f + m_b) * g).astype(BF16))
                col_f.append(ci_f)
                col_b.append(ci_b)
            e_f = jnp.exp2(jnp.where(lo_half, col_f[0], col_f[1]))
            e_b = jnp.exp2(jnp.where(lo_half, col_b[0], col_b[1]))
            y_ref[r, pair(p)] += (_dot(jnp.concatenate(ms, axis=1), x_st[c, :, pair(p)])
                                  + e_f * off_f[:, pair(p)] + e_b * off_b[:, pair(p)])
        return carry

    lax.fori_loop(0, nc, main, 0, unroll=2)


def _ssd_call(xbc, dt, alog, dskip, h0f, h0b, *, seq_len):
    t = xbc.shape[0]
    n_seq = max(1, SSD_ROWS // seq_len)
    blk = n_seq * seq_len
    nc = blk // CHUNK
    has_h0 = h0f is not None
    gx = SSM_INNER // LANES
    gc = gx + SSM_GROUPS
    st_spec = pl.BlockSpec((n_seq, PAIRS_PER_GROUP, LANES, SSM_STATE), lambda s, g: (s, g, 0, 0))
    in_specs = [
        pl.BlockSpec((blk, GROUP_W), lambda s, g: (s, g)),
        pl.BlockSpec((blk, LANES), lambda s, g: (s, gx + g)),
        pl.BlockSpec((blk, LANES), lambda s, g: (s, gc + g)),
        pl.BlockSpec((blk, LANES), lambda s, g: (s, g)),
        pl.BlockSpec((1, LANES), lambda s, g: (0, g)),
        pl.BlockSpec((1, GROUP_W), lambda s, g: (0, g)),
    ]
    args = [xbc, xbc, xbc, dt, alog, dskip]
    y_shape = jax.ShapeDtypeStruct((t, SSM_INNER), F32)
    y_spec = pl.BlockSpec((blk, GROUP_W), lambda s, g: (s, g))
    st_shape = jax.ShapeDtypeStruct((t // seq_len, SSM_HEADS // 2, LANES, SSM_STATE), F32)
    if has_h0:
        in_specs += [st_spec, st_spec]
        args += [h0f, h0b]
        out_specs, out_shape = [y_spec], [y_shape]
    else:
        out_specs, out_shape = [y_spec, st_spec, st_spec], [y_shape, st_shape, st_shape]
    chunk_states = (nc, SSM_STATE, GROUP_W)
    return pl.pallas_call(
        functools.partial(_ssd_kernel, n_seq=n_seq, seq_len=seq_len, has_h0=has_h0),
        grid=(t // blk, SSM_GROUPS),
        in_specs=in_specs,
        out_specs=out_specs,
        out_shape=out_shape,
        scratch_shapes=[
            pltpu.VMEM((nc, 2 * CHUNK, GROUP_W), BF16),
            pltpu.VMEM((blk, LANES), BF16),
            pltpu.VMEM((blk, LANES), BF16),
            pltpu.VMEM((nc, 2 * HEADS_PER_GROUP, CHUNK), F32),
            pltpu.VMEM((nc, 2 * HEADS_PER_GROUP, CHUNK), F32),
            pltpu.VMEM((nc, 2 * HEADS_PER_GROUP, CHUNK), F32),
            pltpu.VMEM((nc, 2 * HEADS_PER_GROUP, CHUNK), F32),
            pltpu.VMEM(chunk_states, F32),
            pltpu.VMEM(chunk_states, F32),
        ],
        compiler_params=_params(("parallel", "arbitrary")),
        name="ssd_latent" if has_h0 else "ssd_context",
    )(*args)


def _gatednorm_kernel(u_ref, y_ref, wz_ref, nw_ref, o_ref, t_scr, ss_scr):
    j = pl.program_id(1)
    nj = SSM_INNER // TN_1
    t = y_ref[...] * _silu(_dot_nt(u_ref[...], wz_ref[...]))
    t_scr[j] = t
    part = jnp.sum(t * t, axis=-1, keepdims=True)

    @pl.when(j == 0)
    def _():
        ss_scr[...] = part

    @pl.when(j > 0)
    def _():
        ss_scr[...] += part

    @pl.when(j == nj - 1)
    def _():
        def slab(r):
            inv = lax.rsqrt(ss_scr[r, :] * (1.0 / SSM_INNER) + EPS)
            for k in range(nj):
                sl = slice(k * TN_1, (k + 1) * TN_1)
                o_ref[r, sl] = (t_scr[k, r, :] * inv * nw_ref[:, sl]).astype(BF16)

        _for_row_slabs(o_ref.shape[0], slab)


def _gatednorm_call(u, y, w_in_t, nw):
    t = u.shape[0]
    nj = SSM_INNER // TN_1
    return pl.pallas_call(
        _gatednorm_kernel,
        grid=(t // TM_1, nj),
        in_specs=[
            pl.BlockSpec((TM_1, D_MODEL), lambda i, j: (i, 0)),
            pl.BlockSpec((TM_1, TN_1), lambda i, j: (i, j)),
            _w_in_rows("z", TN_1),
            pl.BlockSpec((1, SSM_INNER), lambda i, j: (0, 0)),
        ],
        out_specs=pl.BlockSpec((TM_1, SSM_INNER), lambda i, j: (i, 0)),
        out_shape=jax.ShapeDtypeStruct((t, SSM_INNER), BF16),
        scratch_shapes=[pltpu.VMEM((nj, TM_1, TN_1), F32), pltpu.VMEM((TM_1, 1), F32)],
        compiler_params=_params(("parallel", "arbitrary")),
        name="gated_norm",
    )(u, y, w_in_t, nw)


def _merge_kernel(u_ref, cv_ref, yn_ref, wgc_ref, wco_ref, wgs_ref, wso_ref, o_ref):
    u = u_ref[...]
    m = jax.nn.sigmoid(_dot_nt(u, wgc_ref[...])) * _dot(cv_ref[...], wco_ref[...])
    m += jax.nn.sigmoid(_dot_nt(u, wgs_ref[...])) * _dot(yn_ref[...], wso_ref[...])
    o_ref[...] = m.astype(BF16)


def _merge_call(u, cv, yn, wgc_t, wco, wgs_t, wso):
    t = u.shape[0]
    row = pl.BlockSpec((TM, D_MODEL), lambda i, j: (i, 0))
    wspec = pl.BlockSpec((D_MODEL, TN), lambda i, j: (0, j))
    wspec_t = pl.BlockSpec((TN, D_MODEL), lambda i, j: (j, 0))
    return pl.pallas_call(
        _merge_kernel,
        grid=(t // TM, D_MODEL // TN),
        in_specs=[row, row, row, wspec_t, wspec, wspec_t, wspec],
        out_specs=pl.BlockSpec((TM, TN), lambda i, j: (i, j)),
        out_shape=jax.ShapeDtypeStruct((t, D_MODEL), BF16),
        compiler_params=_params(("parallel", "arbitrary")),
        name="merge",
    )(u, cv, yn, wgc_t, wco, wgs_t, wso)


def _group_lanes(f, b):
    lead = f.shape[:-1]
    fb = jnp.concatenate([f.reshape(lead + (SSM_GROUPS, HEADS_PER_GROUP)),
                          b.reshape(lead + (SSM_GROUPS, HEADS_PER_GROUP))], axis=-1)
    pad = [(0, 0)] * (fb.ndim - 1) + [(0, LANES - 2 * HEADS_PER_GROUP)]
    return jnp.pad(fb, pad).reshape(lead + (SSM_GROUPS * LANES,))


def _trunk_layer(h, mod, wts, *, seq_len, row_len, h0f, h0b):
    h1, u2 = _ffn_call(h, mod, wts["norm1"], wts["f1g"], wts["f1u"], wts["f1d"], wts["norm2"], mod_base=0)
    cv, dt = _convproj_call(u2, wts["w_in_t"], wts["conv_w"], wts["w_dt_t"], wts["dt_bias"], row_len=row_len)
    xbc = _xbcproj_call(u2, wts["w_in_t"], wts["ssm_conv_w"], wts["ssm_conv_b"], seq_len=seq_len)
    ssd = _ssd_call(xbc, dt, wts["a_log"], wts["d_skip"], h0f, h0b, seq_len=seq_len)
    yn = _gatednorm_call(u2, ssd[0], wts["w_in_t"], wts["ssm_norm"])
    merged = _merge_call(u2, cv, yn, wts["w_gc_t"], wts["w_co"], wts["w_gs_t"], wts["w_so"])
    out = _ffn_call(h1, mod, wts["norm3"], wts["f2g"], wts["f2u"], wts["f2d"], wts["final_norm"],
                    merged, wts["w_o"], mod_base=6)[0]
    return out, ssd[1:]


def kernel(x_prompt, x_sample, c, state_ssm_fwd, state_ssm_bwd, c_ctx, w_ada, b_ada, norm1_w, ffn1_w_gate, ffn1_w_up, ffn1_w_down, norm2_w, w_in, conv_w, ssm_conv_w, ssm_conv_b, dt_bias_f, dt_bias_b, a_log_f, a_log_b, d_skip, ssm_norm_w, w_conv_out, w_ssm_out, w_o, norm3_w, ffn2_w_gate, ffn2_w_up, ffn2_w_down, final_norm_w):
    depth = w_ada.shape[0]
    assert depth == 1, "the second ffn call is fused with the final norm, so exactly one layer is supported"
    nb, n_ctx, d = x_prompt.shape
    ndec, n_lat, _ = x_sample.shape
    hp = x_prompt.reshape(nb * n_ctx, d)
    hs = x_sample.reshape(ndec * n_lat, d)
    cvec = jnp.zeros((8, d), F32).at[0].set(c_ctx).at[1:1 + ndec].set(c)
    vec = lambda v: v.reshape(1, -1)
    bf = lambda w: w.astype(BF16)
    new_f, new_b = [], []
    for l in range(depth):
        mod = _mod_call(cvec, w_ada[l], b_ada[l]).reshape(8, N_MOD, d)
        w_in_t = bf(w_in[l].T)
        rows = lambda name: w_in_t[IN_OFFSET[name]:IN_OFFSET[name] + IN_WIDTH[name]]
        wts = dict(
            w_in_t=w_in_t, w_gc_t=rows("gc"), w_gs_t=rows("gs"),
            w_dt_t=_group_lanes(rows("dtf").T, rows("dtb").T).T,
            dt_bias=vec(_group_lanes(dt_bias_f[l], dt_bias_b[l])),
            a_log=vec(_group_lanes(a_log_f[l], a_log_b[l])),
            d_skip=vec(jnp.repeat(d_skip[l], SSM_HEADDIM)),
            norm1=vec(norm1_w[l]), norm2=vec(norm2_w[l]), norm3=vec(norm3_w[l]), ssm_norm=vec(ssm_norm_w[l]),
            final_norm=vec(final_norm_w),
            f1g=bf(ffn1_w_gate[l]), f1u=bf(ffn1_w_up[l]), f1d=bf(ffn1_w_down[l]),
            f2g=bf(ffn2_w_gate[l]), f2u=bf(ffn2_w_up[l]), f2d=bf(ffn2_w_down[l]),
            conv_w=conv_w[l], ssm_conv_w=ssm_conv_w[l], ssm_conv_b=vec(ssm_conv_b[l]),
            w_co=bf(w_conv_out[l]), w_so=bf(w_ssm_out[l]), w_o=bf(w_o[l]),
        )
        pair = lambda s: s[:, l].reshape(ndec, SSM_HEADS // 2, 2 * SSM_HEADDIM, SSM_STATE)

        hp, (hf, hb) = _trunk_layer(hp, mod[0:1], wts, seq_len=n_ctx, row_len=n_ctx, h0f=None, h0b=None)
        hs, _ = _trunk_layer(hs, mod[1:1 + ndec], wts, seq_len=n_lat, row_len=GRID_W,
                             h0f=pair(state_ssm_fwd), h0b=pair(state_ssm_bwd))
        new_f.append(hf.reshape(nb, SSM_HEADS, SSM_HEADDIM, SSM_STATE))
        new_b.append(hb.reshape(nb, SSM_HEADS, SSM_HEADDIM, SSM_STATE))
    return (hp.reshape(nb, n_ctx, d), hs.reshape(ndec, n_lat, d),
            jnp.stack(new_f, axis=1), jnp.stack(new_b, axis=1))
```

```python
import functools

import jax
import jax.numpy as jnp
from jax import lax
from jax.experimental import pallas as pl
from jax.experimental.pallas import tpu as pltpu

D_MODEL = 2048
D_FF = 5632
GRID_W = 64
CONV_W = 2048
SSM_INNER = 2048
SSM_HEADDIM = 64
SSM_HEADS = SSM_INNER // SSM_HEADDIM
SSM_GROUPS = 4
SSM_STATE = 128
CHUNK = 128
N_MOD = 9
EPS = 1e-6
XBC_W = SSM_INNER + 2 * SSM_GROUPS * SSM_STATE

IN_SPLITS = (("cb", CONV_W), ("cc", CONV_W), ("cx", CONV_W), ("z", SSM_INNER), ("xbc", XBC_W),
             ("dtf", SSM_HEADS), ("dtb", SSM_HEADS), ("gc", D_MODEL), ("gs", D_MODEL))
IN_OFFSET = {}
for _name, _width in IN_SPLITS:
    IN_OFFSET[_name] = sum(w for _, w in IN_SPLITS[:len(IN_OFFSET)])
IN_WIDTH = dict(IN_SPLITS)

HEADS_PER_GROUP = SSM_HEADS // SSM_GROUPS
PAIRS_PER_GROUP = HEADS_PER_GROUP // 2
GROUP_W = HEADS_PER_GROUP * SSM_HEADDIM
LANES = 128
NEG_BIG = -1e30
LOG2_E = 1.4426950408889634

TM = 512
TN = 1024
TF = 512
TM_1 = 1024
TN_1 = 1024
TN_ADA = 1024
ROW_SLAB = 16
SSD_ROWS = 1024
VMEM_LIMIT = 56 * 1024 * 1024

F32 = jnp.float32
BF16 = jnp.bfloat16


def _dot(a, b):
    return jnp.dot(a, b, preferred_element_type=F32)


def _dot_nt(a, b):
    return lax.dot_general(a, b, (((1,), (1,)), ((), ())), preferred_element_type=F32)


def _silu(x):
    return x * jax.nn.sigmoid(x)


def _rms(x, w):
    ms = jnp.mean(x * x, axis=-1, keepdims=True)
    return x * lax.rsqrt(ms + EPS) * w


def _for_row_slabs(n_rows, fn, static=False):
    if static:
        for s in range(n_rows // ROW_SLAB):
            fn(slice(s * ROW_SLAB, (s + 1) * ROW_SLAB))
        return

    def body(s, carry):
        fn(pl.ds(pl.multiple_of(s * ROW_SLAB, ROW_SLAB), ROW_SLAB))
        return carry

    lax.fori_loop(0, n_rows // ROW_SLAB, body, 0, unroll=8)


def _params(sem):
    return pltpu.CompilerParams(dimension_semantics=sem, vmem_limit_bytes=VMEM_LIMIT)


def _mod_kernel(c_ref, w_ref, b_ref, o_ref):
    a = _silu(c_ref[...]).astype(BF16)
    o_ref[...] = _dot(a, w_ref[...].astype(BF16)) + b_ref[...]


def _mod_call(cvec, w_ada, b_ada):
    n = w_ada.shape[1]
    return pl.pallas_call(
        _mod_kernel,
        grid=(n // TN_ADA,),
        in_specs=[
            pl.BlockSpec((8, D_MODEL), lambda j: (0, 0)),
            pl.BlockSpec((D_MODEL, TN_ADA), lambda j: (0, j)),
            pl.BlockSpec((1, TN_ADA), lambda j: (0, j)),
        ],
        out_specs=pl.BlockSpec((8, TN_ADA), lambda j: (0, j)),
        out_shape=jax.ShapeDtypeStruct((8, n), F32),
        compiler_params=_params(("arbitrary",)),
        name="adaln_mod",
    )(cvec, w_ada, b_ada.reshape(1, n))


def _ffn_kernel(h_ref, mod_ref, nw_ref, wg_ref, wu_ref, wd_ref, nw2_ref, *rest, mod_base, final):
    if final:
        m_ref, wo_ref, o_ref, u_scr, acc_scr, vec_scr = rest
    else:
        o_ref, u_ref, u_scr, acc_scr, vec_scr = rest
    j = pl.program_id(1)
    nj = pl.num_programs(1)

    def mod_row(k):
        return mod_ref[0, mod_base + k:mod_base + k + 1, :]

    def unit_rms(x):
        return x * lax.rsqrt(jnp.mean(x * x, axis=-1, keepdims=True) + EPS)

    @pl.when(j == 0)
    def _():
        vec_scr[0:1, :] = nw_ref[...] * (1.0 + mod_row(1))
        vec_scr[1:2, :] = 0.5 * mod_row(2)
        vec_scr[2:3, :] = nw2_ref[...] if final else nw2_ref[...] * (1.0 + mod_row(4))
        if final:
            acc_scr[...] = _dot(m_ref[...], wo_ref[...])

        def slab(r):
            x = h_ref[r, :]
            if final:
                x = x + mod_row(-1) * acc_scr[r, :]
                o_ref[r, :] = x
            u_scr[r, :] = (unit_rms(x) * vec_scr[0:1, :] + mod_row(0)).astype(BF16)
            acc_scr[r, :] = jnp.zeros((ROW_SLAB, D_MODEL), F32)

        _for_row_slabs(h_ref.shape[0], slab)

    u = u_scr[...]
    a = _silu(_dot(u, wg_ref[...])) * _dot(u, wu_ref[...])
    acc_scr[...] += _dot(a.astype(BF16), wd_ref[...])

    @pl.when(j == nj - 1)
    def _():
        def slab(r):
            if final:
                hn = o_ref[r, :] + vec_scr[1:2, :] * acc_scr[r, :]
                o_ref[r, :] = unit_rms(hn) * vec_scr[2:3, :]
            else:
                hn = h_ref[r, :] + vec_scr[1:2, :] * acc_scr[r, :]
                o_ref[r, :] = hn
                u_ref[r, :] = (unit_rms(hn) * vec_scr[2:3, :] + mod_row(3)).astype(BF16)

        _for_row_slabs(h_ref.shape[0], slab, static=True)


def _ffn_call(h, mod, nw, wg, wu, wd, nw2, merged=None, wo=None, *, mod_base):
    t = h.shape[0]
    final = merged is not None
    tpm = t // mod.shape[0] // TM
    row = pl.BlockSpec((TM, D_MODEL), lambda i, j: (i, 0))
    vec = pl.BlockSpec((1, D_MODEL), lambda i, j: (0, 0))
    in_specs = [
        row,
        pl.BlockSpec((1, N_MOD, D_MODEL), lambda i, j: (i // tpm, 0, 0)),
        vec,
        pl.BlockSpec((D_MODEL, TF), lambda i, j: (0, j)),
        pl.BlockSpec((D_MODEL, TF), lambda i, j: (0, j)),
        pl.BlockSpec((TF, D_MODEL), lambda i, j: (j, 0)),
        vec,
    ]
    args = [h, mod, nw, wg, wu, wd, nw2]
    out_shape = [jax.ShapeDtypeStruct((t, D_MODEL), F32)]
    out_specs = [row]
    if final:
        in_specs += [row, pl.BlockSpec((D_MODEL, D_MODEL), lambda i, j: (0, 0), pipeline_mode=pl.Buffered(1))]
        args += [merged, wo]
    else:
        out_shape.append(jax.ShapeDtypeStruct((t, D_MODEL), BF16))
        out_specs.append(row)
    return pl.pallas_call(
        functools.partial(_ffn_kernel, mod_base=mod_base, final=final),
        grid=(t // TM, D_FF // TF),
        in_specs=in_specs,
        out_specs=out_specs,
        out_shape=out_shape,
        scratch_shapes=[pltpu.VMEM((TM, D_MODEL), BF16), pltpu.VMEM((TM, D_MODEL), F32),
                        pltpu.VMEM((8, D_MODEL), F32)],
        compiler_params=_params(("parallel", "arbitrary")),
        name="ffn_final" if final else "ffn",
    )(*args)


def _softplus(x):
    return jnp.maximum(x, 0.0) + jnp.log1p(jnp.exp(-jnp.abs(x)))


def _convproj_kernel(u_ref, wb_ref, wc_ref, wx_ref, cw_ref, wdt_ref, dtb_ref, o_ref, dt_ref, *, row_len):
    u = u_ref[...]

    @pl.when(pl.program_id(1) == 0)
    def _():
        dt_ref[...] = _softplus(_dot_nt(u, wdt_ref[...]) + dtb_ref[...])

    v = _dot_nt(u, wc_ref[...]) * _dot_nt(u, wx_ref[...])
    pos = lax.broadcasted_iota(jnp.int32, v.shape, 0) & (row_len - 1)
    v_prev = jnp.where(pos == 0, 0.0, pltpu.roll(v, 1, 0))
    v_next = jnp.where(pos == row_len - 1, 0.0, pltpu.roll(v, TM - 1, 0))
    cw = cw_ref[...]
    conv = cw[0:1, :] * v_prev + cw[1:2, :] * v + cw[2:3, :] * v_next
    o_ref[...] = (_dot_nt(u, wb_ref[...]) * conv).astype(BF16)


def _w_in_rows(name, tn):
    first = IN_OFFSET[name] // tn
    assert first * tn == IN_OFFSET[name]
    return pl.BlockSpec((tn, D_MODEL), lambda i, j: (first + j, 0))


def _convproj_call(u, w_in_t, conv_w, w_dt_t, dt_bias, *, row_len):
    t = u.shape[0]
    ndt = w_dt_t.shape[0]
    return pl.pallas_call(
        functools.partial(_convproj_kernel, row_len=row_len),
        grid=(t // TM, CONV_W // TN),
        in_specs=[
            pl.BlockSpec((TM, D_MODEL), lambda i, j: (i, 0)),
            _w_in_rows("cb", TN), _w_in_rows("cc", TN), _w_in_rows("cx", TN),
            pl.BlockSpec((3, TN), lambda i, j: (0, j)),
            pl.BlockSpec((ndt, D_MODEL), lambda i, j: (0, 0)),
            pl.BlockSpec((1, ndt), lambda i, j: (0, 0)),
        ],
        out_specs=[
            pl.BlockSpec((TM, TN), lambda i, j: (i, j)),
            pl.BlockSpec((TM, ndt), lambda i, j: (i, 0)),
        ],
        out_shape=[
            jax.ShapeDtypeStruct((t, CONV_W), BF16),
            jax.ShapeDtypeStruct((t, ndt), F32),
        ],
        compiler_params=_params(("parallel", "arbitrary")),
        name="conv_proj",
    )(u, w_in_t, w_in_t, w_in_t, conv_w, w_dt_t, dt_bias)


def _xbcproj_kernel(u_ref, w_ref, cw_ref, cb_ref, o_ref, *, seq_len):
    v = _dot_nt(u_ref[...], w_ref[...])
    rows = v.shape[0]
    pos = lax.broadcasted_iota(jnp.int32, v.shape, 0) & (seq_len - 1)
    v_prev = jnp.where(pos == 0, 0.0, pltpu.roll(v, 1, 0))
    v_next = jnp.where(pos == seq_len - 1, 0.0, pltpu.roll(v, rows - 1, 0))
    cw = cw_ref[...]
    o_ref[...] = _silu(cw[0:1, :] * v_prev + cw[1:2, :] * v + cw[2:3, :] * v_next + cb_ref[...])


def _xbcproj_call(u, w_in_t, conv_w, conv_b, *, seq_len):
    t = u.shape[0]
    tm = max(TM_1, seq_len)
    tn = TN_1 * TM_1 // tm
    assert tm % seq_len == 0 and seq_len & (seq_len - 1) == 0
    return pl.pallas_call(
        functools.partial(_xbcproj_kernel, seq_len=seq_len),
        grid=(t // tm, XBC_W // tn),
        in_specs=[
            pl.BlockSpec((tm, D_MODEL), lambda i, j: (i, 0)),
            _w_in_rows("xbc", tn),
            pl.BlockSpec((3, tn), lambda i, j: (0, j)),
            pl.BlockSpec((1, tn), lambda i, j: (0, j)),
        ],
        out_specs=pl.BlockSpec((tm, tn), lambda i, j: (i, j)),
        out_shape=jax.ShapeDtypeStruct((t, XBC_W), F32),
        compiler_params=_params(("parallel", "arbitrary")),
        name="xbc_proj",
    )(u, w_in_t, conv_w, conv_b)


def _split3(a):
    a1 = a.astype(BF16)
    r1 = a - a1.astype(F32)
    a2 = r1.astype(BF16)
    a3 = (r1 - a2.astype(F32)).astype(BF16)
    return a1, a2, a3


def _ssd_kernel(x_ref, b_ref, c_ref, dt_ref, alog_ref, dsk_ref, *rest, n_seq, seq_len, has_h0, n_cast):
    cast_in, rest = rest[:n_cast], rest[n_cast:]
    if has_h0:
        h0f_ref, h0b_ref, y_ref, *rest = rest
    else:
        y_ref, hf_ref, hb_ref, *rest = rest
    cast_out, scratch = rest[:n_cast], rest[n_cast:]
    for w_ref, w_bf_ref in zip(cast_in, cast_out):
        w_bf_ref[...] = w_ref[...].astype(BF16)
    x_st, b_bf, c_bf, cum_t, dt_t, w_t, dec, st_f, st_b = scratch
    q = CHUNK
    cps = seq_len // q
    nc = n_seq * cps
    hpg = HEADS_PER_GROUP
    nh = 2 * hpg

    lane = lax.broadcasted_iota(jnp.int32, (q, LANES), 1)
    row = lax.broadcasted_iota(jnp.int32, (q, q), 0)
    col = lax.broadcasted_iota(jnp.int32, (q, q), 1)
    tril = row >= col
    triu = row <= col
    lo_half = lane < SSM_HEADDIM
    lo_half_row = lo_half[0:1, :]

    def rows(c):
        return pl.ds(pl.multiple_of(c * q, q), q)

    def pair(p):
        return slice(p * LANES, (p + 1) * LANES)

    def cumsums(c, carry):
        r = rows(c)
        dt = dt_ref[r, :]
        a_row = jnp.where(lane[0:1, :] < nh, -LOG2_E * jnp.exp(alog_ref[...]), 0.0)
        a = dt * a_row
        tril_b = jnp.where(tril, 1.0, 0.0).astype(BF16)
        p3 = _dot(tril_b, jnp.concatenate(_split3(a), axis=1))
        pre = p3[:, 0:LANES] + p3[:, LANES:2 * LANES] + p3[:, 2 * LANES:]
        suf = (pre[q - 1:q, :] - pre) + a
        cm_t = jnp.where(lane < hpg, pre, suf).T[0:nh, :]
        dtt = dt.T[0:nh, :]
        cum_t[c] = cm_t
        dt_t[c] = dtt

        is_fwd = lax.broadcasted_iota(jnp.int32, (nh, q), 0) < hpg
        edge = jnp.where(is_fwd, cm_t[:, q - 1:q], cm_t[:, 0:1])
        w_t[c] = jnp.exp2(edge - cm_t) * dtt
        dec[c] = jnp.exp2(edge)
        return carry

    lax.fori_loop(0, nc, cumsums, 0, unroll=4)

    def own_states(c, carry):
        r = rows(c)
        x = x_ref[r, :]
        bm = b_ref[r, :]
        y_ref[r, :] = dsk_ref[...] * x
        lo_x = (lax.broadcasted_iota(jnp.int32, x.shape, 1) & (LANES - 1)) < SSM_HEADDIM
        x2 = jnp.concatenate([jnp.where(lo_x, x, 0.0), jnp.where(lo_x, 0.0, x)], axis=0).astype(BF16)
        x_st[c] = x2
        b_bf[r, :] = bm.astype(BF16)
        c_bf[r, :] = c_ref[r, :].astype(BF16)
        w = w_t[c]
        b_t = bm.T
        for p in range(PAIRS_PER_GROUP):
            lhs = jnp.concatenate(
                [jnp.concatenate([b_t * w[h:h + 1, :], b_t * w[h + 1:h + 2, :]], axis=1)
                 for h in (2 * p, hpg + 2 * p)], axis=0).astype(BF16)
            own = _dot(lhs, x2[:, pair(p)])
            st_f[c, :, pair(p)] = own[0:SSM_STATE, :]
            st_b[c, :, pair(p)] = own[SSM_STATE:, :]
        return carry

    lax.fori_loop(0, nc, own_states, 0, unroll=2)

    def pair_decay(c, h):
        d = dec[c]
        return jnp.where(lo_half_row, d[h:h + 1, :], d[h + 1:h + 2, :])

    for p in range(PAIRS_PER_GROUP):
        for s in range(n_seq):
            first = s * cps
            if has_h0:
                init = (h0f_ref[s, p].T, h0b_ref[s, p].T)
            else:
                init = (jnp.zeros((SSM_STATE, LANES), F32), jnp.zeros((SSM_STATE, LANES), F32))

            def chain(k, carry, p=p, first=first):
                sf, sb = carry
                cf = first + k
                cb = first + cps - 1 - k
                own_f = st_f[cf, :, pair(p)]
                own_b = st_b[cb, :, pair(p)]
                st_f[cf, :, pair(p)] = sf
                st_b[cb, :, pair(p)] = sb
                return (pair_decay(cf, 2 * p) * sf + own_f, pair_decay(cb, hpg + 2 * p) * sb + own_b)

            sf, sb = lax.fori_loop(0, cps, chain, init)
            if not has_h0:
                hf_ref[s, p] = sf.T
                hb_ref[s, p] = sb.T

    def main(c, carry):
        r = rows(c)
        cm_t = cum_t[c]
        dtt = dt_t[c]
        cb = c_bf[r, :]
        g = _dot_nt(cb, b_bf[r, :])
        off_f = _dot(cb, st_f[c].astype(BF16))
        off_b = _dot(cb, st_b[c].astype(BF16))
        for p in range(PAIRS_PER_GROUP):
            ms, col_f, col_b = [], [], []
            for e in range(2):
                hf = 2 * p + e
                hb = hpg + hf
                ci_f = jnp.broadcast_to(cm_t[hf:hf + 1, :], (q, q)).T
                ci_b = jnp.broadcast_to(cm_t[hb:hb + 1, :], (q, q)).T
                m_f = jnp.exp2(jnp.where(tril, ci_f - cm_t[hf:hf + 1, :], NEG_BIG)) * dtt[hf:hf + 1, :]
                m_b = jnp.exp2(jnp.where(triu, ci_b - cm_t[hb:hb + 1, :], NEG_BIG)) * dtt[hb:hb + 1, :]
                ms.append(((m_f + m_b) * g).astype(BF16))
                col_f.append(ci_f)
                col_b.append(ci_b)
            e_f = jnp.exp2(jnp.where(lo_half, col_f[0], col_f[1]))
            e_b = jnp.exp2(jnp.where(lo_half, col_b[0], col_b[1]))
            y_ref[r, pair(p)] += (_dot(jnp.concatenate(ms, axis=1), x_st[c, :, pair(p)])
                                  + e_f * off_f[:, pair(p)] + e_b * off_b[:, pair(p)])
        return carry

    lax.fori_loop(0, nc, main, 0, unroll=2)


def _ssd_call(xbc, dt, alog, dskip, h0f, h0b, *, seq_len, casts=()):
    t = xbc.shape[0]
    n_seq = max(1, SSD_ROWS // seq_len)
    blk = n_seq * seq_len
    nc = blk // CHUNK
    has_h0 = h0f is not None
    gx = SSM_INNER // LANES
    gc = gx + SSM_GROUPS
    st_spec = pl.BlockSpec((n_seq, PAIRS_PER_GROUP, LANES, SSM_STATE), lambda s, g: (s, g, 0, 0))
    in_specs = [
        pl.BlockSpec((blk, GROUP_W), lambda s, g: (s, g)),
        pl.BlockSpec((blk, LANES), lambda s, g: (s, gx + g)),
        pl.BlockSpec((blk, LANES), lambda s, g: (s, gc + g)),
        pl.BlockSpec((blk, LANES), lambda s, g: (s, g)),
        pl.BlockSpec((1, LANES), lambda s, g: (0, g)),
        pl.BlockSpec((1, GROUP_W), lambda s, g: (0, g)),
    ]
    args = [xbc, xbc, xbc, dt, alog, dskip]
    n_steps = (t // blk) * SSM_GROUPS
    cast_specs = []
    for w in casts:
        rows_per_step = w.shape[0] // n_steps
        assert rows_per_step * n_steps == w.shape[0] and rows_per_step % 16 == 0
        cast_specs.append(pl.BlockSpec((rows_per_step, w.shape[1]), lambda s, g: (s * SSM_GROUPS + g, 0)))
    in_specs += cast_specs
    args += list(casts)
    y_shape = jax.ShapeDtypeStruct((t, SSM_INNER), F32)
    y_spec = pl.BlockSpec((blk, GROUP_W), lambda s, g: (s, g))
    st_shape = jax.ShapeDtypeStruct((t // seq_len, SSM_HEADS // 2, LANES, SSM_STATE), F32)
    if has_h0:
        in_specs += [st_spec, st_spec]
        args += [h0f, h0b]
        out_specs, out_shape = [y_spec], [y_shape]
    else:
        out_specs, out_shape = [y_spec, st_spec, st_spec], [y_shape, st_shape, st_shape]
    out_specs += cast_specs
    out_shape += [jax.ShapeDtypeStruct(w.shape, BF16) for w in casts]
    chunk_states = (nc, SSM_STATE, GROUP_W)
    return pl.pallas_call(
        functools.partial(_ssd_kernel, n_seq=n_seq, seq_len=seq_len, has_h0=has_h0, n_cast=len(casts)),
        grid=(t // blk, SSM_GROUPS),
        in_specs=in_specs,
        out_specs=out_specs,
        out_shape=out_shape,
        scratch_shapes=[
            pltpu.VMEM((nc, 2 * CHUNK, GROUP_W), BF16),
            pltpu.VMEM((blk, LANES), BF16),
            pltpu.VMEM((blk, LANES), BF16),
            pltpu.VMEM((nc, 2 * HEADS_PER_GROUP, CHUNK), F32),
            pltpu.VMEM((nc, 2 * HEADS_PER_GROUP, CHUNK), F32),
            pltpu.VMEM((nc, 2 * HEADS_PER_GROUP, CHUNK), F32),
            pltpu.VMEM((nc, 2 * HEADS_PER_GROUP, CHUNK), F32),
            pltpu.VMEM(chunk_states, F32),
            pltpu.VMEM(chunk_states, F32),
        ],
        compiler_params=_params(("parallel", "arbitrary")),
        name="ssd_latent" if has_h0 else "ssd_context",
    )(*args)


def _gatednorm_kernel(u_ref, y_ref, wz_ref, nw_ref, o_ref, t_scr, ss_scr):
    j = pl.program_id(1)
    nj = SSM_INNER // TN_1
    t = y_ref[...] * _silu(_dot_nt(u_ref[...], wz_ref[...]))
    t_scr[j] = t
    part = jnp.sum(t * t, axis=-1, keepdims=True)

    @pl.when(j == 0)
    def _():
        ss_scr[...] = part

    @pl.when(j > 0)
    def _():
        ss_scr[...] += part

    @pl.when(j == nj - 1)
    def _():
        def slab(r):
            inv = lax.rsqrt(ss_scr[r, :] * (1.0 / SSM_INNER) + EPS)
            for k in range(nj):
                sl = slice(k * TN_1, (k + 1) * TN_1)
                o_ref[r, sl] = (t_scr[k, r, :] * inv * nw_ref[:, sl]).astype(BF16)

        _for_row_slabs(o_ref.shape[0], slab)


def _gatednorm_call(u, y, w_in_t, nw):
    t = u.shape[0]
    nj = SSM_INNER // TN_1
    return pl.pallas_call(
        _gatednorm_kernel,
        grid=(t // TM_1, nj),
        in_specs=[
            pl.BlockSpec((TM_1, D_MODEL), lambda i, j: (i, 0)),
            pl.BlockSpec((TM_1, TN_1), lambda i, j: (i, j)),
            _w_in_rows("z", TN_1),
            pl.BlockSpec((1, SSM_INNER), lambda i, j: (0, 0)),
        ],
        out_specs=pl.BlockSpec((TM_1, SSM_INNER), lambda i, j: (i, 0)),
        out_shape=jax.ShapeDtypeStruct((t, SSM_INNER), BF16),
        scratch_shapes=[pltpu.VMEM((nj, TM_1, TN_1), F32), pltpu.VMEM((TM_1, 1), F32)],
        compiler_params=_params(("parallel", "arbitrary")),
        name="gated_norm",
    )(u, y, w_in_t, nw)


def _merge_kernel(u_ref, cv_ref, yn_ref, wgc_ref, wco_ref, wgs_ref, wso_ref, o_ref):
    u = u_ref[...]
    m = jax.nn.sigmoid(_dot_nt(u, wgc_ref[...])) * _dot(cv_ref[...], wco_ref[...])
    m += jax.nn.sigmoid(_dot_nt(u, wgs_ref[...])) * _dot(yn_ref[...], wso_ref[...])
    o_ref[...] = m.astype(BF16)


def _merge_call(u, cv, yn, wgc_t, wco, wgs_t, wso):
    t = u.shape[0]
    row = pl.BlockSpec((TM, D_MODEL), lambda i, j: (i, 0))
    wspec = pl.BlockSpec((D_MODEL, TN), lambda i, j: (0, j))
    wspec_t = pl.BlockSpec((TN, D_MODEL), lambda i, j: (j, 0))
    return pl.pallas_call(
        _merge_kernel,
        grid=(t // TM, D_MODEL // TN),
        in_specs=[row, row, row, wspec_t, wspec, wspec_t, wspec],
        out_specs=pl.BlockSpec((TM, TN), lambda i, j: (i, j)),
        out_shape=jax.ShapeDtypeStruct((t, D_MODEL), BF16),
        compiler_params=_params(("parallel", "arbitrary")),
        name="merge",
    )(u, cv, yn, wgc_t, wco, wgs_t, wso)


def _group_lanes(f, b):
    lead = f.shape[:-1]
    fb = jnp.concatenate([f.reshape(lead + (SSM_GROUPS, HEADS_PER_GROUP)),
                          b.reshape(lead + (SSM_GROUPS, HEADS_PER_GROUP))], axis=-1)
    pad = [(0, 0)] * (fb.ndim - 1) + [(0, LANES - 2 * HEADS_PER_GROUP)]
    return jnp.pad(fb, pad).reshape(lead + (SSM_GROUPS * LANES,))


LATE_WEIGHTS = ("f2g", "f2u", "f2d")


def _trunk_layer(h, mod, wts, late, *, seq_len, row_len, h0f, h0b):
    h1, u2 = _ffn_call(h, mod, wts["norm1"], wts["f1g"], wts["f1u"], wts["f1d"], wts["norm2"], mod_base=0)
    cv, dt = _convproj_call(u2, wts["w_in_t"], wts["conv_w"], wts["w_dt_t"], wts["dt_bias"], row_len=row_len)
    xbc = _xbcproj_call(u2, wts["w_in_t"], wts["ssm_conv_w"], wts["ssm_conv_b"], seq_len=seq_len)
    casts = () if late is not None else tuple(wts[k] for k in LATE_WEIGHTS)
    ssd = _ssd_call(xbc, dt, wts["a_log"], wts["d_skip"], h0f, h0b, seq_len=seq_len, casts=casts)
    if late is None:
        late = dict(zip(LATE_WEIGHTS, ssd[len(ssd) - len(LATE_WEIGHTS):]))
        ssd = ssd[:len(ssd) - len(LATE_WEIGHTS)]
    yn = _gatednorm_call(u2, ssd[0], wts["w_in_t"], wts["ssm_norm"])
    merged = _merge_call(u2, cv, yn, wts["w_gc_t"], wts["w_co"], wts["w_gs_t"], wts["w_so"])
    out = _ffn_call(h1, mod, wts["norm3"], late["f2g"], late["f2u"], late["f2d"], wts["final_norm"],
                    merged, wts["w_o"], mod_base=6)[0]
    return out, ssd[1:], late


def kernel(x_prompt, x_sample, c, state_ssm_fwd, state_ssm_bwd, c_ctx, w_ada, b_ada, norm1_w, ffn1_w_gate, ffn1_w_up, ffn1_w_down, norm2_w, w_in, conv_w, ssm_conv_w, ssm_conv_b, dt_bias_f, dt_bias_b, a_log_f, a_log_b, d_skip, ssm_norm_w, w_conv_out, w_ssm_out, w_o, norm3_w, ffn2_w_gate, ffn2_w_up, ffn2_w_down, final_norm_w):
    depth = w_ada.shape[0]
    assert depth == 1, "the second ffn call is fused with the final norm, so exactly one layer is supported"
    nb, n_ctx, d = x_prompt.shape
    ndec, n_lat, _ = x_sample.shape
    hp = x_prompt.reshape(nb * n_ctx, d)
    hs = x_sample.reshape(ndec * n_lat, d)
    cvec = jnp.zeros((8, d), F32).at[0].set(c_ctx).at[1:1 + ndec].set(c)
    vec = lambda v: v.reshape(1, -1)
    bf = lambda w: w.astype(BF16)
    new_f, new_b = [], []
    for l in range(depth):
        mod = _mod_call(cvec, w_ada[l], b_ada[l]).reshape(8, N_MOD, d)
        w_in_t = bf(w_in[l].T)
        rows = lambda name: w_in_t[IN_OFFSET[name]:IN_OFFSET[name] + IN_WIDTH[name]]
        wts = dict(
            w_in_t=w_in_t, w_gc_t=rows("gc"), w_gs_t=rows("gs"),
            w_dt_t=_group_lanes(rows("dtf").T, rows("dtb").T).T,
            dt_bias=vec(_group_lanes(dt_bias_f[l], dt_bias_b[l])),
            a_log=vec(_group_lanes(a_log_f[l], a_log_b[l])),
            d_skip=vec(jnp.repeat(d_skip[l], SSM_HEADDIM)),
            norm1=vec(norm1_w[l]), norm2=vec(norm2_w[l]), norm3=vec(norm3_w[l]), ssm_norm=vec(ssm_norm_w[l]),
            final_norm=vec(final_norm_w),
            f1g=bf(ffn1_w_gate[l]), f1u=bf(ffn1_w_up[l]), f1d=bf(ffn1_w_down[l]),
            conv_w=conv_w[l], ssm_conv_w=ssm_conv_w[l], ssm_conv_b=vec(ssm_conv_b[l]),
            f2g=ffn2_w_gate[l], f2u=ffn2_w_up[l], f2d=ffn2_w_down[l],
            w_co=bf(w_conv_out[l]), w_so=bf(w_ssm_out[l]), w_o=bf(w_o[l]),
        )
        pair = lambda s: s[:, l].reshape(ndec, SSM_HEADS // 2, 2 * SSM_HEADDIM, SSM_STATE)

        hp, (hf, hb), late = _trunk_layer(hp, mod[0:1], wts, None, seq_len=n_ctx, row_len=n_ctx, h0f=None, h0b=None)
        hs, _, _ = _trunk_layer(hs, mod[1:1 + ndec], wts, late, seq_len=n_lat, row_len=GRID_W,
                                h0f=pair(state_ssm_fwd), h0b=pair(state_ssm_bwd))
        new_f.append(hf.reshape(nb, SSM_HEADS, SSM_HEADDIM, SSM_STATE))
        new_b.append(hb.reshape(nb, SSM_HEADS, SSM_HEADDIM, SSM_STATE))
    return (hp.reshape(nb, n_ctx, d), hs.reshape(ndec, n_lat, d),
            jnp.stack(new_f, axis=1), jnp.stack(new_b, axis=1))
```

```python
import functools

import jax
import jax.numpy as jnp
from jax import lax
from jax.experimental import pallas as pl
from jax.experimental.pallas import tpu as pltpu

D_MODEL = 2048
D_FF = 5632
GRID_W = 64
CONV_W = 2048
SSM_INNER = 2048
SSM_HEADDIM = 64
SSM_HEADS = SSM_INNER // SSM_HEADDIM
SSM_GROUPS = 4
SSM_STATE = 128
CHUNK = 128
N_MOD = 9
EPS = 1e-6
XBC_W = SSM_INNER + 2 * SSM_GROUPS * SSM_STATE

IN_SPLITS = (("cb", CONV_W), ("cc", CONV_W), ("cx", CONV_W), ("z", SSM_INNER), ("xbc", XBC_W),
             ("dtf", SSM_HEADS), ("dtb", SSM_HEADS), ("gc", D_MODEL), ("gs", D_MODEL))
IN_OFFSET = {}
for _name, _width in IN_SPLITS:
    IN_OFFSET[_name] = sum(w for _, w in IN_SPLITS[:len(IN_OFFSET)])

HEADS_PER_GROUP = SSM_HEADS // SSM_GROUPS
PAIRS_PER_GROUP = HEADS_PER_GROUP // 2
GROUP_W = HEADS_PER_GROUP * SSM_HEADDIM
LANES = 128
NEG_BIG = -1e30
LOG2_E = 1.4426950408889634

TM = 512
TN = 1024
TF = 512
TM_1 = 1024
TN_1 = 1024
TN_ADA = 1024
ROW_SLAB = 16
SSD_ROWS = 1024
VMEM_LIMIT = 56 * 1024 * 1024

F32 = jnp.float32
BF16 = jnp.bfloat16


def _dot(a, b):
    return jnp.dot(a, b, preferred_element_type=F32)


def _dot_nt(a, b):
    return lax.dot_general(a, b, (((1,), (1,)), ((), ())), preferred_element_type=F32)


def _silu(x):
    return x * jax.nn.sigmoid(x)


def _rms(x, w):
    ms = jnp.mean(x * x, axis=-1, keepdims=True)
    return x * lax.rsqrt(ms + EPS) * w


def _for_row_slabs(n_rows, fn, static=False):
    if static:
        for s in range(n_rows // ROW_SLAB):
            fn(slice(s * ROW_SLAB, (s + 1) * ROW_SLAB))
        return

    def body(s, carry):
        fn(pl.ds(pl.multiple_of(s * ROW_SLAB, ROW_SLAB), ROW_SLAB))
        return carry

    lax.fori_loop(0, n_rows // ROW_SLAB, body, 0, unroll=8)


def _side_cast_specs(casts, grid):
    n_steps = grid[0] * grid[1]
    in_specs, out_specs, shapes = [], [], []
    for w, first_row, n_rows, block_rows in casts:
        n_blocks = n_rows // block_rows
        first = first_row // block_rows
        assert n_blocks * block_rows == n_rows and first * block_rows == first_row
        assert block_rows % 16 == 0 and n_blocks <= n_steps and first_row + n_rows <= w.shape[0]

        def block(i, j, n_blocks=n_blocks):
            return jnp.minimum(i * grid[1] + j, n_blocks - 1)

        in_specs.append(pl.BlockSpec((block_rows, w.shape[1]), lambda i, j, b=block, f=first: (f + b(i, j), 0)))
        out_specs.append(pl.BlockSpec((block_rows, w.shape[1]), lambda i, j, b=block: (b(i, j), 0)))
        shapes.append(jax.ShapeDtypeStruct((n_rows, w.shape[1]), BF16))
    return in_specs, out_specs, shapes


def _convert_blocks(f32_refs, bf16_refs):
    for src, dst in zip(f32_refs, bf16_refs):
        dst[...] = src[...].astype(BF16)


def _params(sem):
    return pltpu.CompilerParams(dimension_semantics=sem, vmem_limit_bytes=VMEM_LIMIT)


def _mod_kernel(c_ref, w_ref, b_ref, o_ref):
    a = _silu(c_ref[...]).astype(BF16)
    o_ref[...] = _dot(a, w_ref[...].astype(BF16)) + b_ref[...]


def _mod_call(cvec, w_ada, b_ada):
    n = w_ada.shape[1]
    return pl.pallas_call(
        _mod_kernel,
        grid=(n // TN_ADA,),
        in_specs=[
            pl.BlockSpec((8, D_MODEL), lambda j: (0, 0)),
            pl.BlockSpec((D_MODEL, TN_ADA), lambda j: (0, j)),
            pl.BlockSpec((1, TN_ADA), lambda j: (0, j)),
        ],
        out_specs=pl.BlockSpec((8, TN_ADA), lambda j: (0, j)),
        out_shape=jax.ShapeDtypeStruct((8, n), F32),
        compiler_params=_params(("arbitrary",)),
        name="adaln_mod",
    )(cvec, w_ada, b_ada.reshape(1, n))


def _ffn_kernel(h_ref, mod_ref, nw_ref, wg_ref, wu_ref, wd_ref, nw2_ref, *rest, mod_base, final, n_cast):
    if final:
        m_ref, wo_ref, *rest = rest
    cast_in, rest = rest[:n_cast], rest[n_cast:]
    if final:
        o_ref, *rest = rest
    else:
        o_ref, u_ref, *rest = rest
    cast_out, (u_scr, acc_scr, vec_scr) = rest[:n_cast], rest[n_cast:]
    _convert_blocks(cast_in, cast_out)
    j = pl.program_id(1)
    nj = pl.num_programs(1)

    def mod_row(k):
        return mod_ref[0, mod_base + k:mod_base + k + 1, :]

    def unit_rms(x):
        return x * lax.rsqrt(jnp.mean(x * x, axis=-1, keepdims=True) + EPS)

    @pl.when(j == 0)
    def _():
        vec_scr[0:1, :] = nw_ref[...] * (1.0 + mod_row(1))
        vec_scr[1:2, :] = 0.5 * mod_row(2)
        vec_scr[2:3, :] = nw2_ref[...] if final else nw2_ref[...] * (1.0 + mod_row(4))
        if final:
            acc_scr[...] = _dot(m_ref[...], wo_ref[...])

        def slab(r):
            x = h_ref[r, :]
            if final:
                x = x + mod_row(-1) * acc_scr[r, :]
                o_ref[r, :] = x
            u_scr[r, :] = (unit_rms(x) * vec_scr[0:1, :] + mod_row(0)).astype(BF16)
            acc_scr[r, :] = jnp.zeros((ROW_SLAB, D_MODEL), F32)

        _for_row_slabs(h_ref.shape[0], slab)

    u = u_scr[...]
    a = _silu(_dot(u, wg_ref[...])) * _dot(u, wu_ref[...])
    acc_scr[...] += _dot(a.astype(BF16), wd_ref[...])

    @pl.when(j == nj - 1)
    def _():
        def slab(r):
            if final:
                hn = o_ref[r, :] + vec_scr[1:2, :] * acc_scr[r, :]
                o_ref[r, :] = unit_rms(hn) * vec_scr[2:3, :]
            else:
                hn = h_ref[r, :] + vec_scr[1:2, :] * acc_scr[r, :]
                o_ref[r, :] = hn
                u_ref[r, :] = (unit_rms(hn) * vec_scr[2:3, :] + mod_row(3)).astype(BF16)

        _for_row_slabs(h_ref.shape[0], slab, static=True)


def _ffn_call(h, mod, nw, wg, wu, wd, nw2, merged=None, wo=None, *, mod_base, casts=()):
    t = h.shape[0]
    final = merged is not None
    grid = (t // TM, D_FF // TF)
    cast_in_specs, cast_out_specs, cast_shapes = _side_cast_specs(casts, grid)
    tpm = t // mod.shape[0] // TM
    row = pl.BlockSpec((TM, D_MODEL), lambda i, j: (i, 0))
    vec = pl.BlockSpec((1, D_MODEL), lambda i, j: (0, 0))
    in_specs = [
        row,
        pl.BlockSpec((1, N_MOD, D_MODEL), lambda i, j: (i // tpm, 0, 0)),
        vec,
        pl.BlockSpec((D_MODEL, TF), lambda i, j: (0, j)),
        pl.BlockSpec((D_MODEL, TF), lambda i, j: (0, j)),
        pl.BlockSpec((TF, D_MODEL), lambda i, j: (j, 0)),
        vec,
    ]
    args = [h, mod, nw, wg, wu, wd, nw2]
    out_shape = [jax.ShapeDtypeStruct((t, D_MODEL), F32)]
    out_specs = [row]
    if final:
        in_specs += [row, pl.BlockSpec((D_MODEL, D_MODEL), lambda i, j: (0, 0), pipeline_mode=pl.Buffered(1))]
        args += [merged, wo]
    else:
        out_shape.append(jax.ShapeDtypeStruct((t, D_MODEL), BF16))
        out_specs.append(row)
    return pl.pallas_call(
        functools.partial(_ffn_kernel, mod_base=mod_base, final=final, n_cast=len(casts)),
        grid=grid,
        in_specs=in_specs + cast_in_specs,
        out_specs=out_specs + cast_out_specs,
        out_shape=out_shape + cast_shapes,
        scratch_shapes=[pltpu.VMEM((TM, D_MODEL), BF16), pltpu.VMEM((TM, D_MODEL), F32),
                        pltpu.VMEM((8, D_MODEL), F32)],
        compiler_params=_params(("arbitrary" if casts else "parallel", "arbitrary")),
        name="ffn_final" if final else "ffn",
    )(*args, *[entry[0] for entry in casts])


def _softplus(x):
    return jnp.maximum(x, 0.0) + jnp.log1p(jnp.exp(-jnp.abs(x)))


def _convproj_kernel(u_ref, wb_ref, wc_ref, wx_ref, cw_ref, wdt_ref, dtb_ref, *rest, row_len, n_cast):
    cast_in, (o_ref, dt_ref), cast_out = rest[:n_cast], rest[n_cast:n_cast + 2], rest[n_cast + 2:]
    _convert_blocks(cast_in, cast_out)
    u = u_ref[...]

    @pl.when(pl.program_id(1) == 0)
    def _():
        dt_ref[...] = _softplus(_dot_nt(u, wdt_ref[...]) + dtb_ref[...])

    v = _dot_nt(u, wc_ref[...]) * _dot_nt(u, wx_ref[...])
    pos = lax.broadcasted_iota(jnp.int32, v.shape, 0) & (row_len - 1)
    v_prev = jnp.where(pos == 0, 0.0, pltpu.roll(v, 1, 0))
    v_next = jnp.where(pos == row_len - 1, 0.0, pltpu.roll(v, TM - 1, 0))
    cw = cw_ref[...]
    conv = cw[0:1, :] * v_prev + cw[1:2, :] * v + cw[2:3, :] * v_next
    o_ref[...] = (_dot_nt(u, wb_ref[...]) * conv).astype(BF16)


def _w_in_rows(name, tn):
    first = IN_OFFSET[name] // tn
    assert first * tn == IN_OFFSET[name]
    return pl.BlockSpec((tn, D_MODEL), lambda i, j: (first + j, 0))


def _convproj_call(u, w_in_t, conv_w, w_dt_t, dt_bias, *, row_len, casts=()):
    t = u.shape[0]
    ndt = w_dt_t.shape[0]
    grid = (t // TM, CONV_W // TN)
    cast_in_specs, cast_out_specs, cast_shapes = _side_cast_specs(casts, grid)
    return pl.pallas_call(
        functools.partial(_convproj_kernel, row_len=row_len, n_cast=len(casts)),
        grid=grid,
        in_specs=[
            pl.BlockSpec((TM, D_MODEL), lambda i, j: (i, 0)),
            _w_in_rows("cb", TN), _w_in_rows("cc", TN), _w_in_rows("cx", TN),
            pl.BlockSpec((3, TN), lambda i, j: (0, j)),
            pl.BlockSpec((ndt, D_MODEL), lambda i, j: (0, 0)),
            pl.BlockSpec((1, ndt), lambda i, j: (0, 0)),
        ] + cast_in_specs,
        out_specs=[
            pl.BlockSpec((TM, TN), lambda i, j: (i, j)),
            pl.BlockSpec((TM, ndt), lambda i, j: (i, 0)),
        ] + cast_out_specs,
        out_shape=[
            jax.ShapeDtypeStruct((t, CONV_W), BF16),
            jax.ShapeDtypeStruct((t, ndt), F32),
        ] + cast_shapes,
        compiler_params=_params(("parallel", "arbitrary")),
        name="conv_proj",
    )(u, w_in_t, w_in_t, w_in_t, conv_w, w_dt_t, dt_bias, *[entry[0] for entry in casts])


def _xbcproj_kernel(u_ref, w_ref, cw_ref, cb_ref, o_ref, *, seq_len):
    v = _dot_nt(u_ref[...], w_ref[...])
    rows = v.shape[0]
    pos = lax.broadcasted_iota(jnp.int32, v.shape, 0) & (seq_len - 1)
    v_prev = jnp.where(pos == 0, 0.0, pltpu.roll(v, 1, 0))
    v_next = jnp.where(pos == seq_len - 1, 0.0, pltpu.roll(v, rows - 1, 0))
    cw = cw_ref[...]
    o_ref[...] = _silu(cw[0:1, :] * v_prev + cw[1:2, :] * v + cw[2:3, :] * v_next + cb_ref[...])


def _xbcproj_call(u, w_in_t, conv_w, conv_b, *, seq_len):
    t = u.shape[0]
    tm = max(TM_1, seq_len)
    tn = TN_1 * TM_1 // tm
    assert tm % seq_len == 0 and seq_len & (seq_len - 1) == 0
    return pl.pallas_call(
        functools.partial(_xbcproj_kernel, seq_len=seq_len),
        grid=(t // tm, XBC_W // tn),
        in_specs=[
            pl.BlockSpec((tm, D_MODEL), lambda i, j: (i, 0)),
            _w_in_rows("xbc", tn),
            pl.BlockSpec((3, tn), lambda i, j: (0, j)),
            pl.BlockSpec((1, tn), lambda i, j: (0, j)),
        ],
        out_specs=pl.BlockSpec((tm, tn), lambda i, j: (i, j)),
        out_shape=jax.ShapeDtypeStruct((t, XBC_W), F32),
        compiler_params=_params(("parallel", "arbitrary")),
        name="xbc_proj",
    )(u, w_in_t, conv_w, conv_b)


def _split3(a):
    a1 = a.astype(BF16)
    r1 = a - a1.astype(F32)
    a2 = r1.astype(BF16)
    a3 = (r1 - a2.astype(F32)).astype(BF16)
    return a1, a2, a3


def _ssd_kernel(x_ref, b_ref, c_ref, dt_ref, alog_ref, dsk_ref, *rest, n_seq, seq_len, has_h0, n_cast):
    cast_in, rest = rest[:n_cast], rest[n_cast:]
    if has_h0:
        h0f_ref, h0b_ref, y_ref, *rest = rest
    else:
        y_ref, hf_ref, hb_ref, *rest = rest
    cast_out, scratch = rest[:n_cast], rest[n_cast:]
    _convert_blocks(cast_in, cast_out)
    x_st, b_bf, c_bf, cum_t, dt_t, w_t, dec, st_f, st_b = scratch
    q = CHUNK
    cps = seq_len // q
    nc = n_seq * cps
    hpg = HEADS_PER_GROUP
    nh = 2 * hpg

    lane = lax.broadcasted_iota(jnp.int32, (q, LANES), 1)
    row = lax.broadcasted_iota(jnp.int32, (q, q), 0)
    col = lax.broadcasted_iota(jnp.int32, (q, q), 1)
    tril = row >= col
    triu = row <= col
    lo_half = lane < SSM_HEADDIM
    lo_half_row = lo_half[0:1, :]

    def rows(c):
        return pl.ds(pl.multiple_of(c * q, q), q)

    def pair(p):
        return slice(p * LANES, (p + 1) * LANES)

    def cumsums(c, carry):
        r = rows(c)
        dt = dt_ref[r, :]
        a_row = jnp.where(lane[0:1, :] < nh, -LOG2_E * jnp.exp(alog_ref[...]), 0.0)
        a = dt * a_row
        tril_b = jnp.where(tril, 1.0, 0.0).astype(BF16)
        p3 = _dot(tril_b, jnp.concatenate(_split3(a), axis=1))
        pre = p3[:, 0:LANES] + p3[:, LANES:2 * LANES] + p3[:, 2 * LANES:]
        suf = (pre[q - 1:q, :] - pre) + a
        cm_t = jnp.where(lane < hpg, pre, suf).T[0:nh, :]
        dtt = dt.T[0:nh, :]
        cum_t[c] = cm_t
        dt_t[c] = dtt

        is_fwd = lax.broadcasted_iota(jnp.int32, (nh, q), 0) < hpg
        edge = jnp.where(is_fwd, cm_t[:, q - 1:q], cm_t[:, 0:1])
        w_t[c] = jnp.exp2(edge - cm_t) * dtt
        dec[c] = jnp.exp2(edge)
        return carry

    lax.fori_loop(0, nc, cumsums, 0, unroll=4)

    def own_states(c, carry):
        r = rows(c)
        x = x_ref[r, :]
        bm = b_ref[r, :]
        y_ref[r, :] = dsk_ref[...] * x
        lo_x = (lax.broadcasted_iota(jnp.int32, x.shape, 1) & (LANES - 1)) < SSM_HEADDIM
        x2 = jnp.concatenate([jnp.where(lo_x, x, 0.0), jnp.where(lo_x, 0.0, x)], axis=0).astype(BF16)
        x_st[c] = x2
        b_bf[r, :] = bm.astype(BF16)
        c_bf[r, :] = c_ref[r, :].astype(BF16)
        w = w_t[c]
        b_t = bm.T
        for p in range(PAIRS_PER_GROUP):
            lhs = jnp.concatenate(
                [jnp.concatenate([b_t * w[h:h + 1, :], b_t * w[h + 1:h + 2, :]], axis=1)
                 for h in (2 * p, hpg + 2 * p)], axis=0).astype(BF16)
            own = _dot(lhs, x2[:, pair(p)])
            st_f[c, :, pair(p)] = own[0:SSM_STATE, :]
            st_b[c, :, pair(p)] = own[SSM_STATE:, :]
        return carry

    lax.fori_loop(0, nc, own_states, 0, unroll=2)

    def pair_decay(c, h):
        d = dec[c]
        return jnp.where(lo_half_row, d[h:h + 1, :], d[h + 1:h + 2, :])

    for p in range(PAIRS_PER_GROUP):
        for s in range(n_seq):
            first = s * cps
            if has_h0:
                init = (h0f_ref[s, p].T, h0b_ref[s, p].T)
            else:
                init = (jnp.zeros((SSM_STATE, LANES), F32), jnp.zeros((SSM_STATE, LANES), F32))

            def chain(k, carry, p=p, first=first):
                sf, sb = carry
                cf = first + k
                cb = first + cps - 1 - k
                own_f = st_f[cf, :, pair(p)]
                own_b = st_b[cb, :, pair(p)]
                st_f[cf, :, pair(p)] = sf
                st_b[cb, :, pair(p)] = sb
                return (pair_decay(cf, 2 * p) * sf + own_f, pair_decay(cb, hpg + 2 * p) * sb + own_b)

            sf, sb = lax.fori_loop(0, cps, chain, init)
            if not has_h0:
                hf_ref[s, p] = sf.T
                hb_ref[s, p] = sb.T

    def main(c, carry):
        r = rows(c)
        cm_t = cum_t[c]
        dtt = dt_t[c]
        cb = c_bf[r, :]
        g = _dot_nt(cb, b_bf[r, :])
        off_f = _dot(cb, st_f[c].astype(BF16))
        off_b = _dot(cb, st_b[c].astype(BF16))
        for p in range(PAIRS_PER_GROUP):
            ms, col_f, col_b = [], [], []
            for e in range(2):
                hf = 2 * p + e
                hb = hpg + hf
                ci_f = jnp.broadcast_to(cm_t[hf:hf + 1, :], (q, q)).T
                ci_b = jnp.broadcast_to(cm_t[hb:hb + 1, :], (q, q)).T
                m_f = jnp.exp2(jnp.where(tril, ci_f - cm_t[hf:hf + 1, :], NEG_BIG)) * dtt[hf:hf + 1, :]
                m_b = jnp.exp2(jnp.where(triu, ci_b - cm_t[hb:hb + 1, :], NEG_BIG)) * dtt[hb:hb + 1, :]
                ms.append(((m_f + m_b) * g).astype(BF16))
                col_f.append(ci_f)
                col_b.append(ci_b)
            e_f = jnp.exp2(jnp.where(lo_half, col_f[0], col_f[1]))
            e_b = jnp.exp2(jnp.where(lo_half, col_b[0], col_b[1]))
            y_ref[r, pair(p)] += (_dot(jnp.concatenate(ms, axis=1), x_st[c, :, pair(p)])
                                  + e_f * off_f[:, pair(p)] + e_b * off_b[:, pair(p)])
        return carry

    lax.fori_loop(0, nc, main, 0, unroll=2)


def _ssd_call(xbc, dt, alog, dskip, h0f, h0b, *, seq_len, casts=()):
    t = xbc.shape[0]
    n_seq = max(1, SSD_ROWS // seq_len)
    blk = n_seq * seq_len
    nc = blk // CHUNK
    has_h0 = h0f is not None
    gx = SSM_INNER // LANES
    gc = gx + SSM_GROUPS
    st_spec = pl.BlockSpec((n_seq, PAIRS_PER_GROUP, LANES, SSM_STATE), lambda s, g: (s, g, 0, 0))
    in_specs = [
        pl.BlockSpec((blk, GROUP_W), lambda s, g: (s, g)),
        pl.BlockSpec((blk, LANES), lambda s, g: (s, gx + g)),
        pl.BlockSpec((blk, LANES), lambda s, g: (s, gc + g)),
        pl.BlockSpec((blk, LANES), lambda s, g: (s, g)),
        pl.BlockSpec((1, LANES), lambda s, g: (0, g)),
        pl.BlockSpec((1, GROUP_W), lambda s, g: (0, g)),
    ]
    args = [xbc, xbc, xbc, dt, alog, dskip]
    grid = (t // blk, SSM_GROUPS)
    cast_in_specs, cast_out_specs, cast_shapes = _side_cast_specs(casts, grid)
    in_specs += cast_in_specs
    args += [entry[0] for entry in casts]
    y_shape = jax.ShapeDtypeStruct((t, SSM_INNER), F32)
    y_spec = pl.BlockSpec((blk, GROUP_W), lambda s, g: (s, g))
    st_shape = jax.ShapeDtypeStruct((t // seq_len, SSM_HEADS // 2, LANES, SSM_STATE), F32)
    if has_h0:
        in_specs += [st_spec, st_spec]
        args += [h0f, h0b]
        out_specs, out_shape = [y_spec], [y_shape]
    else:
        out_specs, out_shape = [y_spec, st_spec, st_spec], [y_shape, st_shape, st_shape]
    out_specs += cast_out_specs
    out_shape += cast_shapes
    chunk_states = (nc, SSM_STATE, GROUP_W)
    return pl.pallas_call(
        functools.partial(_ssd_kernel, n_seq=n_seq, seq_len=seq_len, has_h0=has_h0, n_cast=len(casts)),
        grid=grid,
        in_specs=in_specs,
        out_specs=out_specs,
        out_shape=out_shape,
        scratch_shapes=[
            pltpu.VMEM((nc, 2 * CHUNK, GROUP_W), BF16),
            pltpu.VMEM((blk, LANES), BF16),
            pltpu.VMEM((blk, LANES), BF16),
            pltpu.VMEM((nc, 2 * HEADS_PER_GROUP, CHUNK), F32),
            pltpu.VMEM((nc, 2 * HEADS_PER_GROUP, CHUNK), F32),
            pltpu.VMEM((nc, 2 * HEADS_PER_GROUP, CHUNK), F32),
            pltpu.VMEM((nc, 2 * HEADS_PER_GROUP, CHUNK), F32),
            pltpu.VMEM(chunk_states, F32),
            pltpu.VMEM(chunk_states, F32),
        ],
        compiler_params=_params(("parallel", "arbitrary")),
        name="ssd_latent" if has_h0 else "ssd_context",
    )(*args)


def _gatednorm_kernel(u_ref, y_ref, wz_ref, nw_ref, o_ref, t_scr, ss_scr):
    j = pl.program_id(1)
    nj = SSM_INNER // TN_1
    t = y_ref[...] * _silu(_dot_nt(u_ref[...], wz_ref[...]))
    t_scr[j] = t
    part = jnp.sum(t * t, axis=-1, keepdims=True)

    @pl.when(j == 0)
    def _():
        ss_scr[...] = part

    @pl.when(j > 0)
    def _():
        ss_scr[...] += part

    @pl.when(j == nj - 1)
    def _():
        def slab(r):
            inv = lax.rsqrt(ss_scr[r, :] * (1.0 / SSM_INNER) + EPS)
            for k in range(nj):
                sl = slice(k * TN_1, (k + 1) * TN_1)
                o_ref[r, sl] = (t_scr[k, r, :] * inv * nw_ref[:, sl]).astype(BF16)

        _for_row_slabs(o_ref.shape[0], slab)


def _gatednorm_call(u, y, w_in_t, nw):
    t = u.shape[0]
    nj = SSM_INNER // TN_1
    return pl.pallas_call(
        _gatednorm_kernel,
        grid=(t // TM_1, nj),
        in_specs=[
            pl.BlockSpec((TM_1, D_MODEL), lambda i, j: (i, 0)),
            pl.BlockSpec((TM_1, TN_1), lambda i, j: (i, j)),
            _w_in_rows("z", TN_1),
            pl.BlockSpec((1, SSM_INNER), lambda i, j: (0, 0)),
        ],
        out_specs=pl.BlockSpec((TM_1, SSM_INNER), lambda i, j: (i, 0)),
        out_shape=jax.ShapeDtypeStruct((t, SSM_INNER), BF16),
        scratch_shapes=[pltpu.VMEM((nj, TM_1, TN_1), F32), pltpu.VMEM((TM_1, 1), F32)],
        compiler_params=_params(("parallel", "arbitrary")),
        name="gated_norm",
    )(u, y, w_in_t, nw)


def _merge_kernel(u_ref, cv_ref, yn_ref, wgc_ref, wco_ref, wgs_ref, wso_ref, o_ref):
    u = u_ref[...]
    m = jax.nn.sigmoid(_dot_nt(u, wgc_ref[...])) * _dot(cv_ref[...], wco_ref[...])
    m += jax.nn.sigmoid(_dot_nt(u, wgs_ref[...])) * _dot(yn_ref[...], wso_ref[...])
    o_ref[...] = m.astype(BF16)


def _merge_call(u, cv, yn, w_gate_t, wco, wso):
    t = u.shape[0]
    row = pl.BlockSpec((TM, D_MODEL), lambda i, j: (i, 0))
    wspec = pl.BlockSpec((D_MODEL, TN), lambda i, j: (0, j))
    wspec_gc = pl.BlockSpec((TN, D_MODEL), lambda i, j: (j, 0))
    wspec_gs = pl.BlockSpec((TN, D_MODEL), lambda i, j: (D_MODEL // TN + j, 0))
    return pl.pallas_call(
        _merge_kernel,
        grid=(t // TM, D_MODEL // TN),
        in_specs=[row, row, row, wspec_gc, wspec, wspec_gs, wspec],
        out_specs=pl.BlockSpec((TM, TN), lambda i, j: (i, j)),
        out_shape=jax.ShapeDtypeStruct((t, D_MODEL), BF16),
        compiler_params=_params(("parallel", "arbitrary")),
        name="merge",
    )(u, cv, yn, w_gate_t, wco, w_gate_t, wso)


def _group_lanes(f, b):
    lead = f.shape[:-1]
    fb = jnp.concatenate([f.reshape(lead + (SSM_GROUPS, HEADS_PER_GROUP)),
                          b.reshape(lead + (SSM_GROUPS, HEADS_PER_GROUP))], axis=-1)
    pad = [(0, 0)] * (fb.ndim - 1) + [(0, LANES - 2 * HEADS_PER_GROUP)]
    return jnp.pad(fb, pad).reshape(lead + (SSM_GROUPS * LANES,))


SIDE_CASTS = {
    "ffn": {"w_in_t": ("w_in_t", 0, IN_OFFSET["dtf"], 128),
            "w_dt_rows": ("w_in_t", IN_OFFSET["dtf"], 2 * SSM_HEADS, 64),
            "w_gate_t": ("w_in_t", IN_OFFSET["gc"], 2 * D_MODEL, 64)},
    "conv_proj": {k: (k, 0, D_MODEL, 128) for k in ("w_co", "w_so", "w_o")},
    "ssd": {"f2g": ("f2g", 0, D_MODEL, 128), "f2u": ("f2u", 0, D_MODEL, 128), "f2d": ("f2d", 0, D_FF, 352)},
}


def _trunk_layer(h, mod, wts, shared, *, seq_len, row_len, h0f, h0b):
    produce = shared is None
    shared = {} if produce else shared

    def casts(host):
        return tuple((wts[src], *where) for src, *where in SIDE_CASTS[host].values()) if produce else ()

    def split(host, outs):
        n = len(SIDE_CASTS[host]) if produce else 0
        shared.update(zip(SIDE_CASTS[host], outs[len(outs) - n:]))
        return outs[:len(outs) - n]

    h1, u2 = split("ffn", _ffn_call(h, mod, wts["norm1"], wts["f1g"], wts["f1u"], wts["f1d"], wts["norm2"],
                                    mod_base=0, casts=casts("ffn")))
    w_in_t = shared["w_in_t"]
    dt_rows = shared["w_dt_rows"]
    w_dt_t = _group_lanes(dt_rows[:SSM_HEADS].T, dt_rows[SSM_HEADS:].T).T
    cv, dt = split("conv_proj", _convproj_call(u2, w_in_t, wts["conv_w"], w_dt_t, wts["dt_bias"],
                                               row_len=row_len, casts=casts("conv_proj")))
    xbc = _xbcproj_call(u2, w_in_t, wts["ssm_conv_w"], wts["ssm_conv_b"], seq_len=seq_len)
    ssd = split("ssd", _ssd_call(xbc, dt, wts["a_log"], wts["d_skip"], h0f, h0b, seq_len=seq_len, casts=casts("ssd")))
    yn = _gatednorm_call(u2, ssd[0], w_in_t, wts["ssm_norm"])
    merged = _merge_call(u2, cv, yn, shared["w_gate_t"], shared["w_co"], shared["w_so"])
    out = _ffn_call(h1, mod, wts["norm3"], shared["f2g"], shared["f2u"], shared["f2d"], wts["final_norm"],
                    merged, shared["w_o"], mod_base=6)[0]
    return out, ssd[1:], shared


def kernel(x_prompt, x_sample, c, state_ssm_fwd, state_ssm_bwd, c_ctx, w_ada, b_ada, norm1_w, ffn1_w_gate, ffn1_w_up, ffn1_w_down, norm2_w, w_in, conv_w, ssm_conv_w, ssm_conv_b, dt_bias_f, dt_bias_b, a_log_f, a_log_b, d_skip, ssm_norm_w, w_conv_out, w_ssm_out, w_o, norm3_w, ffn2_w_gate, ffn2_w_up, ffn2_w_down, final_norm_w):
    depth = w_ada.shape[0]
    assert depth == 1, "the second ffn call is fused with the final norm, so exactly one layer is supported"
    nb, n_ctx, d = x_prompt.shape
    ndec, n_lat, _ = x_sample.shape
    hp = x_prompt.reshape(nb * n_ctx, d)
    hs = x_sample.reshape(ndec * n_lat, d)
    cvec = jnp.zeros((8, d), F32).at[0].set(c_ctx).at[1:1 + ndec].set(c)
    vec = lambda v: v.reshape(1, -1)
    bf = lambda w: w.astype(BF16)
    new_f, new_b = [], []
    for l in range(depth):
        mod = _mod_call(cvec, w_ada[l], b_ada[l]).reshape(8, N_MOD, d)
        w_in_t = w_in[l].T
        wts = dict(
            w_in_t=w_in_t,
            dt_bias=vec(_group_lanes(dt_bias_f[l], dt_bias_b[l])),
            a_log=vec(_group_lanes(a_log_f[l], a_log_b[l])),
            d_skip=vec(jnp.repeat(d_skip[l], SSM_HEADDIM)),
            norm1=vec(norm1_w[l]), norm2=vec(norm2_w[l]), norm3=vec(norm3_w[l]), ssm_norm=vec(ssm_norm_w[l]),
            final_norm=vec(final_norm_w),
            f1g=bf(ffn1_w_gate[l]), f1u=bf(ffn1_w_up[l]), f1d=bf(ffn1_w_down[l]),
            conv_w=conv_w[l], ssm_conv_w=ssm_conv_w[l], ssm_conv_b=vec(ssm_conv_b[l]),
            f2g=ffn2_w_gate[l], f2u=ffn2_w_up[l], f2d=ffn2_w_down[l],
            w_co=w_conv_out[l], w_so=w_ssm_out[l], w_o=w_o[l],
        )
        pair = lambda s: s[:, l].reshape(ndec, SSM_HEADS // 2, 2 * SSM_HEADDIM, SSM_STATE)

        hp, (hf, hb), shared = _trunk_layer(hp, mod[0:1], wts, None, seq_len=n_ctx, row_len=n_ctx, h0f=None, h0b=None)
        hs, _, _ = _trunk_layer(hs, mod[1:1 + ndec], wts, shared, seq_len=n_lat, row_len=GRID_W,
                                h0f=pair(state_ssm_fwd), h0b=pair(state_ssm_bwd))
        new_f.append(hf.reshape(nb, SSM_HEADS, SSM_HEADDIM, SSM_STATE))
        new_b.append(hb.reshape(nb, SSM_HEADS, SSM_HEADDIM, SSM_STATE))
    return (hp.reshape(nb, n_ctx, d), hs.reshape(ndec, n_lat, d),
            jnp.stack(new_f, axis=1), jnp.stack(new_b, axis=1))
```

```python
import functools

import jax
import jax.numpy as jnp
from jax import lax
from jax.experimental import pallas as pl
from jax.experimental.pallas import tpu as pltpu

D_MODEL = 2048
D_FF = 5632
GRID_W = 64
CONV_W = 2048
SSM_INNER = 2048
SSM_HEADDIM = 64
SSM_HEADS = SSM_INNER // SSM_HEADDIM
SSM_GROUPS = 4
SSM_STATE = 128
CHUNK = 128
N_MOD = 9
EPS = 1e-6
XBC_W = SSM_INNER + 2 * SSM_GROUPS * SSM_STATE

IN_SPLITS = (("cb", CONV_W), ("cc", CONV_W), ("cx", CONV_W), ("z", SSM_INNER), ("xbc", XBC_W),
             ("dtf", SSM_HEADS), ("dtb", SSM_HEADS), ("gc", D_MODEL), ("gs", D_MODEL))
IN_OFFSET = {}
for _name, _width in IN_SPLITS:
    IN_OFFSET[_name] = sum(w for _, w in IN_SPLITS[:len(IN_OFFSET)])

HEADS_PER_GROUP = SSM_HEADS // SSM_GROUPS
PAIRS_PER_GROUP = HEADS_PER_GROUP // 2
GROUP_W = HEADS_PER_GROUP * SSM_HEADDIM
LANES = 128
NEG_BIG = -1e30
LOG2_E = 1.4426950408889634

TM = 512
TN = 1024
TF = 512
TM_1 = 1024
TN_1 = 1024
TN_ADA = 1024
ROW_SLAB = 16
SSD_ROWS = 1024
VMEM_LIMIT = 56 * 1024 * 1024

F32 = jnp.float32
BF16 = jnp.bfloat16


def _dot(a, b):
    return jnp.dot(a, b, preferred_element_type=F32)


def _dot_nt(a, b):
    return lax.dot_general(a, b, (((1,), (1,)), ((), ())), preferred_element_type=F32)


def _silu(x):
    return x * jax.nn.sigmoid(x)


def _rms(x, w):
    ms = jnp.mean(x * x, axis=-1, keepdims=True)
    return x * lax.rsqrt(ms + EPS) * w


def _for_row_slabs(n_rows, fn, static=False):
    if static:
        for s in range(n_rows // ROW_SLAB):
            fn(slice(s * ROW_SLAB, (s + 1) * ROW_SLAB))
        return

    def body(s, carry):
        fn(pl.ds(pl.multiple_of(s * ROW_SLAB, ROW_SLAB), ROW_SLAB))
        return carry

    lax.fori_loop(0, n_rows // ROW_SLAB, body, 0, unroll=8)


def _side_cast_specs(casts, grid):
    n_steps = grid[0] * grid[1]
    in_specs, out_specs, shapes = [], [], []
    for w, first_row, n_rows, block_rows, transposed in casts:
        n_blocks = n_rows // block_rows
        first = first_row // block_rows
        assert n_blocks * block_rows == n_rows and first * block_rows == first_row
        assert block_rows % 16 == 0 and n_blocks <= n_steps and first_row + n_rows <= w.shape[0]

        def block(i, j, n_blocks=n_blocks):
            return jnp.minimum(i * grid[1] + j, n_blocks - 1)

        in_specs.append(pl.BlockSpec((block_rows, w.shape[1]), lambda i, j, b=block, f=first: (f + b(i, j), 0)))
        if transposed:
            assert block_rows == LANES and w.shape[1] % LANES == 0
            out_specs.append(pl.BlockSpec((w.shape[1], block_rows), lambda i, j, b=block: (0, b(i, j))))
            shapes.append(jax.ShapeDtypeStruct((w.shape[1], n_rows), BF16))
        else:
            out_specs.append(pl.BlockSpec((block_rows, w.shape[1]), lambda i, j, b=block: (b(i, j), 0)))
            shapes.append(jax.ShapeDtypeStruct((n_rows, w.shape[1]), BF16))
    return in_specs, out_specs, shapes


def _convert_blocks(f32_refs, bf16_refs):
    for src, dst in zip(f32_refs, bf16_refs):
        if dst.shape == src.shape:
            dst[...] = src[...].astype(BF16)
        else:
            for k in range(src.shape[1] // LANES):
                cols = slice(k * LANES, (k + 1) * LANES)
                dst[cols, :] = src[:, cols].T.astype(BF16)


def _params(sem):
    return pltpu.CompilerParams(dimension_semantics=sem, vmem_limit_bytes=VMEM_LIMIT)


def _mod_kernel(c_ref, w_ref, b_ref, o_ref):
    a = _silu(c_ref[...]).astype(BF16)
    o_ref[...] = _dot(a, w_ref[...].astype(BF16)) + b_ref[...]


def _mod_call(cvec, w_ada, b_ada):
    n = w_ada.shape[1]
    return pl.pallas_call(
        _mod_kernel,
        grid=(n // TN_ADA,),
        in_specs=[
            pl.BlockSpec((8, D_MODEL), lambda j: (0, 0)),
            pl.BlockSpec((D_MODEL, TN_ADA), lambda j: (0, j)),
            pl.BlockSpec((1, TN_ADA), lambda j: (0, j)),
        ],
        out_specs=pl.BlockSpec((8, TN_ADA), lambda j: (0, j)),
        out_shape=jax.ShapeDtypeStruct((8, n), F32),
        compiler_params=_params(("arbitrary",)),
        name="adaln_mod",
    )(cvec, w_ada, b_ada.reshape(1, n))


def _ffn_kernel(h_ref, mod_ref, nw_ref, wg_ref, wu_ref, wd_ref, nw2_ref, *rest, mod_base, final, n_cast):
    if final:
        m_ref, wo_ref, *rest = rest
    cast_in, rest = rest[:n_cast], rest[n_cast:]
    if final:
        o_ref, *rest = rest
    else:
        o_ref, u_ref, *rest = rest
    cast_out, (u_scr, acc_scr, vec_scr) = rest[:n_cast], rest[n_cast:]
    j = pl.program_id(1)
    nj = pl.num_programs(1)

    def mod_row(k):
        return mod_ref[0, mod_base + k:mod_base + k + 1, :]

    def unit_rms(x):
        return x * lax.rsqrt(jnp.mean(x * x, axis=-1, keepdims=True) + EPS)

    @pl.when(j == 0)
    def _():
        vec_scr[0:1, :] = nw_ref[...] * (1.0 + mod_row(1))
        vec_scr[1:2, :] = 0.5 * mod_row(2)
        vec_scr[2:3, :] = nw2_ref[...] if final else nw2_ref[...] * (1.0 + mod_row(4))
        if final:
            acc_scr[...] = _dot(m_ref[...], wo_ref[...])

        def slab(r):
            x = h_ref[r, :]
            if final:
                x = x + mod_row(-1) * acc_scr[r, :]
                o_ref[r, :] = x
            u_scr[r, :] = (unit_rms(x) * vec_scr[0:1, :] + mod_row(0)).astype(BF16)
            acc_scr[r, :] = jnp.zeros((ROW_SLAB, D_MODEL), F32)

        _for_row_slabs(h_ref.shape[0], slab)

    u = u_scr[...]
    a = _silu(_dot(u, wg_ref[...])) * _dot(u, wu_ref[...])
    acc_scr[...] += _dot(a.astype(BF16), wd_ref[...])
    _convert_blocks(cast_in, cast_out)

    @pl.when(j == nj - 1)
    def _():
        def slab(r):
            if final:
                hn = o_ref[r, :] + vec_scr[1:2, :] * acc_scr[r, :]
                o_ref[r, :] = unit_rms(hn) * vec_scr[2:3, :]
            else:
                hn = h_ref[r, :] + vec_scr[1:2, :] * acc_scr[r, :]
                o_ref[r, :] = hn
                u_ref[r, :] = (unit_rms(hn) * vec_scr[2:3, :] + mod_row(3)).astype(BF16)

        _for_row_slabs(h_ref.shape[0], slab, static=True)


def _ffn_call(h, mod, nw, wg, wu, wd, nw2, merged=None, wo=None, *, mod_base, casts=()):
    t = h.shape[0]
    final = merged is not None
    grid = (t // TM, D_FF // TF)
    cast_in_specs, cast_out_specs, cast_shapes = _side_cast_specs(casts, grid)
    tpm = t // mod.shape[0] // TM
    row = pl.BlockSpec((TM, D_MODEL), lambda i, j: (i, 0))
    vec = pl.BlockSpec((1, D_MODEL), lambda i, j: (0, 0))
    in_specs = [
        row,
        pl.BlockSpec((1, N_MOD, D_MODEL), lambda i, j: (i // tpm, 0, 0)),
        vec,
        pl.BlockSpec((D_MODEL, TF), lambda i, j: (0, j)),
        pl.BlockSpec((D_MODEL, TF), lambda i, j: (0, j)),
        pl.BlockSpec((TF, D_MODEL), lambda i, j: (j, 0)),
        vec,
    ]
    args = [h, mod, nw, wg, wu, wd, nw2]
    out_shape = [jax.ShapeDtypeStruct((t, D_MODEL), F32)]
    out_specs = [row]
    if final:
        in_specs += [row, pl.BlockSpec((D_MODEL, D_MODEL), lambda i, j: (0, 0), pipeline_mode=pl.Buffered(1))]
        args += [merged, wo]
    else:
        out_shape.append(jax.ShapeDtypeStruct((t, D_MODEL), BF16))
        out_specs.append(row)
    return pl.pallas_call(
        functools.partial(_ffn_kernel, mod_base=mod_base, final=final, n_cast=len(casts)),
        grid=grid,
        in_specs=in_specs + cast_in_specs,
        out_specs=out_specs + cast_out_specs,
        out_shape=out_shape + cast_shapes,
        scratch_shapes=[pltpu.VMEM((TM, D_MODEL), BF16), pltpu.VMEM((TM, D_MODEL), F32),
                        pltpu.VMEM((8, D_MODEL), F32)],
        compiler_params=_params(("arbitrary" if casts else "parallel", "arbitrary")),
        name="ffn_final" if final else "ffn",
    )(*args, *[entry[0] for entry in casts])


def _softplus(x):
    return jnp.maximum(x, 0.0) + jnp.log1p(jnp.exp(-jnp.abs(x)))


def _convproj_kernel(u_ref, wb_ref, wc_ref, wx_ref, cw_ref, wdt_ref, dtb_ref, *rest, row_len, n_cast):
    cast_in, (o_ref, dt_ref), cast_out = rest[:n_cast], rest[n_cast:n_cast + 2], rest[n_cast + 2:]
    u = u_ref[...]

    @pl.when(pl.program_id(1) == 0)
    def _():
        dt_ref[...] = _softplus(_dot_nt(u, wdt_ref[...]) + dtb_ref[...])

    v = _dot(u, wc_ref[...]) * _dot(u, wx_ref[...])
    pos = lax.broadcasted_iota(jnp.int32, v.shape, 0) & (row_len - 1)
    v_prev = jnp.where(pos == 0, 0.0, pltpu.roll(v, 1, 0))
    v_next = jnp.where(pos == row_len - 1, 0.0, pltpu.roll(v, TM - 1, 0))
    cw = cw_ref[...]
    conv = cw[0:1, :] * v_prev + cw[1:2, :] * v + cw[2:3, :] * v_next
    o_ref[...] = (_dot(u, wb_ref[...]) * conv).astype(BF16)
    _convert_blocks(cast_in, cast_out)


def _w_in_cols(name, tn):
    first = IN_OFFSET[name] // tn
    assert first * tn == IN_OFFSET[name]
    return pl.BlockSpec((D_MODEL, tn), lambda i, j: (0, first + j))


def _convproj_call(u, w_in_bf, conv_w, w_dt_t, dt_bias, *, row_len, casts=()):
    t = u.shape[0]
    ndt = w_dt_t.shape[0]
    grid = (t // TM, CONV_W // TN)
    cast_in_specs, cast_out_specs, cast_shapes = _side_cast_specs(casts, grid)
    return pl.pallas_call(
        functools.partial(_convproj_kernel, row_len=row_len, n_cast=len(casts)),
        grid=grid,
        in_specs=[
            pl.BlockSpec((TM, D_MODEL), lambda i, j: (i, 0)),
            _w_in_cols("cb", TN), _w_in_cols("cc", TN), _w_in_cols("cx", TN),
            pl.BlockSpec((3, TN), lambda i, j: (0, j)),
            pl.BlockSpec((ndt, D_MODEL), lambda i, j: (0, 0)),
            pl.BlockSpec((1, ndt), lambda i, j: (0, 0)),
        ] + cast_in_specs,
        out_specs=[
            pl.BlockSpec((TM, TN), lambda i, j: (i, j)),
            pl.BlockSpec((TM, ndt), lambda i, j: (i, 0)),
        ] + cast_out_specs,
        out_shape=[
            jax.ShapeDtypeStruct((t, CONV_W), BF16),
            jax.ShapeDtypeStruct((t, ndt), F32),
        ] + cast_shapes,
        compiler_params=_params(("parallel", "arbitrary")),
        name="conv_proj",
    )(u, w_in_bf, w_in_bf, w_in_bf, conv_w, w_dt_t, dt_bias, *[entry[0] for entry in casts])


def _xbcproj_kernel(u_ref, w_ref, cw_ref, cb_ref, o_ref, *, seq_len):
    v = _dot(u_ref[...], w_ref[...])
    rows = v.shape[0]
    pos = lax.broadcasted_iota(jnp.int32, v.shape, 0) & (seq_len - 1)
    v_prev = jnp.where(pos == 0, 0.0, pltpu.roll(v, 1, 0))
    v_next = jnp.where(pos == seq_len - 1, 0.0, pltpu.roll(v, rows - 1, 0))
    cw = cw_ref[...]
    o_ref[...] = _silu(cw[0:1, :] * v_prev + cw[1:2, :] * v + cw[2:3, :] * v_next + cb_ref[...])


def _xbcproj_call(u, w_in_bf, conv_w, conv_b, *, seq_len):
    t = u.shape[0]
    tm = max(TM_1, seq_len)
    tn = TN_1 * TM_1 // tm
    assert tm % seq_len == 0 and seq_len & (seq_len - 1) == 0
    return pl.pallas_call(
        functools.partial(_xbcproj_kernel, seq_len=seq_len),
        grid=(t // tm, XBC_W // tn),
        in_specs=[
            pl.BlockSpec((tm, D_MODEL), lambda i, j: (i, 0)),
            _w_in_cols("xbc", tn),
            pl.BlockSpec((3, tn), lambda i, j: (0, j)),
            pl.BlockSpec((1, tn), lambda i, j: (0, j)),
        ],
        out_specs=pl.BlockSpec((tm, tn), lambda i, j: (i, j)),
        out_shape=jax.ShapeDtypeStruct((t, XBC_W), F32),
        compiler_params=_params(("parallel", "arbitrary")),
        name="xbc_proj",
    )(u, w_in_bf, conv_w, conv_b)


def _split3(a):
    a1 = a.astype(BF16)
    r1 = a - a1.astype(F32)
    a2 = r1.astype(BF16)
    a3 = (r1 - a2.astype(F32)).astype(BF16)
    return a1, a2, a3


def _ssd_kernel(x_ref, b_ref, c_ref, dt_ref, alog_ref, dsk_ref, *rest, n_seq, seq_len, has_h0, n_cast):
    cast_in, rest = rest[:n_cast], rest[n_cast:]
    if has_h0:
        h0f_ref, h0b_ref, y_ref, *rest = rest
    else:
        y_ref, hf_ref, hb_ref, *rest = rest
    cast_out, scratch = rest[:n_cast], rest[n_cast:]
    _convert_blocks(cast_in, cast_out)
    x_st, b_bf, c_bf, cum_t, dt_t, w_t, dec, st_f, st_b = scratch
    q = CHUNK
    cps = seq_len // q
    nc = n_seq * cps
    hpg = HEADS_PER_GROUP
    nh = 2 * hpg

    lane = lax.broadcasted_iota(jnp.int32, (q, LANES), 1)
    row = lax.broadcasted_iota(jnp.int32, (q, q), 0)
    col = lax.broadcasted_iota(jnp.int32, (q, q), 1)
    tril = row >= col
    triu = row <= col
    lo_half = lane < SSM_HEADDIM
    lo_half_row = lo_half[0:1, :]

    def rows(c):
        return pl.ds(pl.multiple_of(c * q, q), q)

    def pair(p):
        return slice(p * LANES, (p + 1) * LANES)

    def cumsums(c, carry):
        r = rows(c)
        dt = dt_ref[r, :]
        a_row = jnp.where(lane[0:1, :] < nh, -LOG2_E * jnp.exp(alog_ref[...]), 0.0)
        a = dt * a_row
        tril_b = jnp.where(tril, 1.0, 0.0).astype(BF16)
        p3 = _dot(tril_b, jnp.concatenate(_split3(a), axis=1))
        pre = p3[:, 0:LANES] + p3[:, LANES:2 * LANES] + p3[:, 2 * LANES:]
        suf = (pre[q - 1:q, :] - pre) + a
        cm_t = jnp.where(lane < hpg, pre, suf).T[0:nh, :]
        dtt = dt.T[0:nh, :]
        cum_t[c] = cm_t
        dt_t[c] = dtt

        is_fwd = lax.broadcasted_iota(jnp.int32, (nh, q), 0) < hpg
        edge = jnp.where(is_fwd, cm_t[:, q - 1:q], cm_t[:, 0:1])
        w_t[c] = jnp.exp2(edge - cm_t) * dtt
        dec[c] = jnp.exp2(edge)
        return carry

    lax.fori_loop(0, nc, cumsums, 0, unroll=4)

    def own_states(c, carry):
        r = rows(c)
        x = x_ref[r, :]
        bm = b_ref[r, :]
        y_ref[r, :] = dsk_ref[...] * x
        lo_x = (lax.broadcasted_iota(jnp.int32, x.shape, 1) & (LANES - 1)) < SSM_HEADDIM
        x2 = jnp.concatenate([jnp.where(lo_x, x, 0.0), jnp.where(lo_x, 0.0, x)], axis=0).astype(BF16)
        x_st[c] = x2
        b_bf[r, :] = bm.astype(BF16)
        c_bf[r, :] = c_ref[r, :].astype(BF16)
        w = w_t[c]
        b_t = bm.T
        for p in range(PAIRS_PER_GROUP):
            lhs = jnp.concatenate(
                [jnp.concatenate([b_t * w[h:h + 1, :], b_t * w[h + 1:h + 2, :]], axis=1)
                 for h in (2 * p, hpg + 2 * p)], axis=0).astype(BF16)
            own = _dot(lhs, x2[:, pair(p)])
            st_f[c, :, pair(p)] = own[0:SSM_STATE, :]
            st_b[c, :, pair(p)] = own[SSM_STATE:, :]
        return carry

    lax.fori_loop(0, nc, own_states, 0, unroll=2)

    def pair_decay(c, h):
        d = dec[c]
        return jnp.where(lo_half_row, d[h:h + 1, :], d[h + 1:h + 2, :])

    for p in range(PAIRS_PER_GROUP):
        for s in range(n_seq):
            first = s * cps
            if has_h0:
                init = (h0f_ref[s, p].T, h0b_ref[s, p].T)
            else:
                init = (jnp.zeros((SSM_STATE, LANES), F32), jnp.zeros((SSM_STATE, LANES), F32))

            def chain(k, carry, p=p, first=first):
                sf, sb = carry
                cf = first + k
                cb = first + cps - 1 - k
                own_f = st_f[cf, :, pair(p)]
                own_b = st_b[cb, :, pair(p)]
                st_f[cf, :, pair(p)] = sf
                st_b[cb, :, pair(p)] = sb
                return (pair_decay(cf, 2 * p) * sf + own_f, pair_decay(cb, hpg + 2 * p) * sb + own_b)

            sf, sb = lax.fori_loop(0, cps, chain, init)
            if not has_h0:
                hf_ref[s, p] = sf.T
                hb_ref[s, p] = sb.T

    def main(c, carry):
        r = rows(c)
        cm_t = cum_t[c]
        dtt = dt_t[c]
        cb = c_bf[r, :]
        g = _dot_nt(cb, b_bf[r, :])
        off_f = _dot(cb, st_f[c].astype(BF16))
        off_b = _dot(cb, st_b[c].astype(BF16))
        for p in range(PAIRS_PER_GROUP):
            ms, col_f, col_b = [], [], []
            for e in range(2):
                hf = 2 * p + e
                hb = hpg + hf
                ci_f = jnp.broadcast_to(cm_t[hf:hf + 1, :], (q, q)).T
                ci_b = jnp.broadcast_to(cm_t[hb:hb + 1, :], (q, q)).T
                m_f = jnp.exp2(jnp.where(tril, ci_f - cm_t[hf:hf + 1, :], NEG_BIG)) * dtt[hf:hf + 1, :]
                m_b = jnp.exp2(jnp.where(triu, ci_b - cm_t[hb:hb + 1, :], NEG_BIG)) * dtt[hb:hb + 1, :]
                ms.append(((m_f + m_b) * g).astype(BF16))
                col_f.append(ci_f)
                col_b.append(ci_b)
            e_f = jnp.exp2(jnp.where(lo_half, col_f[0], col_f[1]))
            e_b = jnp.exp2(jnp.where(lo_half, col_b[0], col_b[1]))
            y_ref[r, pair(p)] += (_dot(jnp.concatenate(ms, axis=1), x_st[c, :, pair(p)])
                                  + e_f * off_f[:, pair(p)] + e_b * off_b[:, pair(p)])
        return carry

    lax.fori_loop(0, nc, main, 0, unroll=2)


def _ssd_call(xbc, dt, alog, dskip, h0f, h0b, *, seq_len, casts=()):
    t = xbc.shape[0]
    n_seq = max(1, SSD_ROWS // seq_len)
    blk = n_seq * seq_len
    nc = blk // CHUNK
    has_h0 = h0f is not None
    gx = SSM_INNER // LANES
    gc = gx + SSM_GROUPS
    st_spec = pl.BlockSpec((n_seq, PAIRS_PER_GROUP, LANES, SSM_STATE), lambda s, g: (s, g, 0, 0))
    in_specs = [
        pl.BlockSpec((blk, GROUP_W), lambda s, g: (s, g)),
        pl.BlockSpec((blk, LANES), lambda s, g: (s, gx + g)),
        pl.BlockSpec((blk, LANES), lambda s, g: (s, gc + g)),
        pl.BlockSpec((blk, LANES), lambda s, g: (s, g)),
        pl.BlockSpec((1, LANES), lambda s, g: (0, g)),
        pl.BlockSpec((1, GROUP_W), lambda s, g: (0, g)),
    ]
    args = [xbc, xbc, xbc, dt, alog, dskip]
    grid = (t // blk, SSM_GROUPS)
    cast_in_specs, cast_out_specs, cast_shapes = _side_cast_specs(casts, grid)
    in_specs += cast_in_specs
    args += [entry[0] for entry in casts]
    y_shape = jax.ShapeDtypeStruct((t, SSM_INNER), F32)
    y_spec = pl.BlockSpec((blk, GROUP_W), lambda s, g: (s, g))
    st_shape = jax.ShapeDtypeStruct((t // seq_len, SSM_HEADS // 2, LANES, SSM_STATE), F32)
    if has_h0:
        in_specs += [st_spec, st_spec]
        args += [h0f, h0b]
        out_specs, out_shape = [y_spec], [y_shape]
    else:
        out_specs, out_shape = [y_spec, st_spec, st_spec], [y_shape, st_shape, st_shape]
    out_specs += cast_out_specs
    out_shape += cast_shapes
    chunk_states = (nc, SSM_STATE, GROUP_W)
    return pl.pallas_call(
        functools.partial(_ssd_kernel, n_seq=n_seq, seq_len=seq_len, has_h0=has_h0, n_cast=len(casts)),
        grid=grid,
        in_specs=in_specs,
        out_specs=out_specs,
        out_shape=out_shape,
        scratch_shapes=[
            pltpu.VMEM((nc, 2 * CHUNK, GROUP_W), BF16),
            pltpu.VMEM((blk, LANES), BF16),
            pltpu.VMEM((blk, LANES), BF16),
            pltpu.VMEM((nc, 2 * HEADS_PER_GROUP, CHUNK), F32),
            pltpu.VMEM((nc, 2 * HEADS_PER_GROUP, CHUNK), F32),
            pltpu.VMEM((nc, 2 * HEADS_PER_GROUP, CHUNK), F32),
            pltpu.VMEM((nc, 2 * HEADS_PER_GROUP, CHUNK), F32),
            pltpu.VMEM(chunk_states, F32),
            pltpu.VMEM(chunk_states, F32),
        ],
        compiler_params=_params(("parallel", "arbitrary")),
        name="ssd_latent" if has_h0 else "ssd_context",
    )(*args)


def _gatednorm_kernel(u_ref, y_ref, wz_ref, nw_ref, o_ref, t_scr, ss_scr):
    j = pl.program_id(1)
    nj = SSM_INNER // TN_1
    t = y_ref[...] * _silu(_dot(u_ref[...], wz_ref[...]))
    t_scr[j] = t
    part = jnp.sum(t * t, axis=-1, keepdims=True)

    @pl.when(j == 0)
    def _():
        ss_scr[...] = part

    @pl.when(j > 0)
    def _():
        ss_scr[...] += part

    @pl.when(j == nj - 1)
    def _():
        def slab(r):
            inv = lax.rsqrt(ss_scr[r, :] * (1.0 / SSM_INNER) + EPS)
            for k in range(nj):
                sl = slice(k * TN_1, (k + 1) * TN_1)
                o_ref[r, sl] = (t_scr[k, r, :] * inv * nw_ref[:, sl]).astype(BF16)

        _for_row_slabs(o_ref.shape[0], slab)


def _gatednorm_call(u, y, w_in_bf, nw):
    t = u.shape[0]
    nj = SSM_INNER // TN_1
    return pl.pallas_call(
        _gatednorm_kernel,
        grid=(t // TM_1, nj),
        in_specs=[
            pl.BlockSpec((TM_1, D_MODEL), lambda i, j: (i, 0)),
            pl.BlockSpec((TM_1, TN_1), lambda i, j: (i, j)),
            _w_in_cols("z", TN_1),
            pl.BlockSpec((1, SSM_INNER), lambda i, j: (0, 0)),
        ],
        out_specs=pl.BlockSpec((TM_1, SSM_INNER), lambda i, j: (i, 0)),
        out_shape=jax.ShapeDtypeStruct((t, SSM_INNER), BF16),
        scratch_shapes=[pltpu.VMEM((nj, TM_1, TN_1), F32), pltpu.VMEM((TM_1, 1), F32)],
        compiler_params=_params(("parallel", "arbitrary")),
        name="gated_norm",
    )(u, y, w_in_bf, nw)


def _merge_kernel(u_ref, cv_ref, yn_ref, wgc_ref, wco_ref, wgs_ref, wso_ref, o_ref):
    u = u_ref[...]
    m = jax.nn.sigmoid(_dot_nt(u, wgc_ref[...])) * _dot(cv_ref[...], wco_ref[...])
    m += jax.nn.sigmoid(_dot_nt(u, wgs_ref[...])) * _dot(yn_ref[...], wso_ref[...])
    o_ref[...] = m.astype(BF16)


def _merge_call(u, cv, yn, w_gate_t, wco, wso):
    t = u.shape[0]
    row = pl.BlockSpec((TM, D_MODEL), lambda i, j: (i, 0))
    wspec = pl.BlockSpec((D_MODEL, TN), lambda i, j: (0, j))
    wspec_gc = pl.BlockSpec((TN, D_MODEL), lambda i, j: (j, 0))
    wspec_gs = pl.BlockSpec((TN, D_MODEL), lambda i, j: (D_MODEL // TN + j, 0))
    return pl.pallas_call(
        _merge_kernel,
        grid=(t // TM, D_MODEL // TN),
        in_specs=[row, row, row, wspec_gc, wspec, wspec_gs, wspec],
        out_specs=pl.BlockSpec((TM, TN), lambda i, j: (i, j)),
        out_shape=jax.ShapeDtypeStruct((t, D_MODEL), BF16),
        compiler_params=_params(("parallel", "arbitrary")),
        name="merge",
    )(u, cv, yn, w_gate_t, wco, w_gate_t, wso)


def _group_lanes(f, b):
    lead = f.shape[:-1]
    fb = jnp.concatenate([f.reshape(lead + (SSM_GROUPS, HEADS_PER_GROUP)),
                          b.reshape(lead + (SSM_GROUPS, HEADS_PER_GROUP))], axis=-1)
    pad = [(0, 0)] * (fb.ndim - 1) + [(0, LANES - 2 * HEADS_PER_GROUP)]
    return jnp.pad(fb, pad).reshape(lead + (SSM_GROUPS * LANES,))


SIDE_CASTS = {
    "ffn": {"w_in": ("w_in_t", 0, IN_OFFSET["dtf"], 128, True),
            "w_dt_rows": ("w_in_t", IN_OFFSET["dtf"], 2 * SSM_HEADS, 64, False),
            "w_gate_t": ("w_in_t", IN_OFFSET["gc"], 2 * D_MODEL, 64, False)},
    "conv_proj": {k: (k, 0, D_MODEL, 128, False) for k in ("w_co", "w_so", "w_o")},
    "ssd": {"f2g": ("f2g", 0, D_MODEL, 128, False), "f2u": ("f2u", 0, D_MODEL, 128, False),
            "f2d": ("f2d", 0, D_FF, 352, False)},
}


def _trunk_layer(h, mod, wts, shared, *, seq_len, row_len, h0f, h0b):
    produce = shared is None
    shared = {} if produce else shared

    def casts(host):
        return tuple((wts[src], *where) for src, *where in SIDE_CASTS[host].values()) if produce else ()

    def split(host, outs):
        n = len(SIDE_CASTS[host]) if produce else 0
        shared.update(zip(SIDE_CASTS[host], outs[len(outs) - n:]))
        return outs[:len(outs) - n]

    h1, u2 = split("ffn", _ffn_call(h, mod, wts["norm1"], wts["f1g"], wts["f1u"], wts["f1d"], wts["norm2"],
                                    mod_base=0, casts=casts("ffn")))
    w_in_bf = shared["w_in"]
    dt_rows = shared["w_dt_rows"]
    w_dt_t = _group_lanes(dt_rows[:SSM_HEADS].T, dt_rows[SSM_HEADS:].T).T
    cv, dt = split("conv_proj", _convproj_call(u2, w_in_bf, wts["conv_w"], w_dt_t, wts["dt_bias"],
                                               row_len=row_len, casts=casts("conv_proj")))
    xbc = _xbcproj_call(u2, w_in_bf, wts["ssm_conv_w"], wts["ssm_conv_b"], seq_len=seq_len)
    ssd = split("ssd", _ssd_call(xbc, dt, wts["a_log"], wts["d_skip"], h0f, h0b, seq_len=seq_len, casts=casts("ssd")))
    yn = _gatednorm_call(u2, ssd[0], w_in_bf, wts["ssm_norm"])
    merged = _merge_call(u2, cv, yn, shared["w_gate_t"], shared["w_co"], shared["w_so"])
    out = _ffn_call(h1, mod, wts["norm3"], shared["f2g"], shared["f2u"], shared["f2d"], wts["final_norm"],
                    merged, shared["w_o"], mod_base=6)[0]
    return out, ssd[1:], shared


def kernel(x_prompt, x_sample, c, state_ssm_fwd, state_ssm_bwd, c_ctx, w_ada, b_ada, norm1_w, ffn1_w_gate, ffn1_w_up, ffn1_w_down, norm2_w, w_in, conv_w, ssm_conv_w, ssm_conv_b, dt_bias_f, dt_bias_b, a_log_f, a_log_b, d_skip, ssm_norm_w, w_conv_out, w_ssm_out, w_o, norm3_w, ffn2_w_gate, ffn2_w_up, ffn2_w_down, final_norm_w):
    depth = w_ada.shape[0]
    assert depth == 1, "the second ffn call is fused with the final norm, so exactly one layer is supported"
    nb, n_ctx, d = x_prompt.shape
    ndec, n_lat, _ = x_sample.shape
    hp = x_prompt.reshape(nb * n_ctx, d)
    hs = x_sample.reshape(ndec * n_lat, d)
    cvec = jnp.zeros((8, d), F32).at[0].set(c_ctx).at[1:1 + ndec].set(c)
    vec = lambda v: v.reshape(1, -1)
    bf = lambda w: w.astype(BF16)
    new_f, new_b = [], []
    for l in range(depth):
        mod = _mod_call(cvec, w_ada[l], b_ada[l]).reshape(8, N_MOD, d)
        w_in_t = w_in[l].T
        wts = dict(
            w_in_t=w_in_t,
            dt_bias=vec(_group_lanes(dt_bias_f[l], dt_bias_b[l])),
            a_log=vec(_group_lanes(a_log_f[l], a_log_b[l])),
            d_skip=vec(jnp.repeat(d_skip[l], SSM_HEADDIM)),
            norm1=vec(norm1_w[l]), norm2=vec(norm2_w[l]), norm3=vec(norm3_w[l]), ssm_norm=vec(ssm_norm_w[l]),
            final_norm=vec(final_norm_w),
            f1g=bf(ffn1_w_gate[l]), f1u=bf(ffn1_w_up[l]), f1d=bf(ffn1_w_down[l]),
            conv_w=conv_w[l], ssm_conv_w=ssm_conv_w[l], ssm_conv_b=vec(ssm_conv_b[l]),
            f2g=ffn2_w_gate[l], f2u=ffn2_w_up[l], f2d=ffn2_w_down[l],
            w_co=w_conv_out[l], w_so=w_ssm_out[l], w_o=w_o[l],
        )
        pair = lambda s: s[:, l].reshape(ndec, SSM_HEADS // 2, 2 * SSM_HEADDIM, SSM_STATE)

        hp, (hf, hb), shared = _trunk_layer(hp, mod[0:1], wts, None, seq_len=n_ctx, row_len=n_ctx, h0f=None, h0b=None)
        hs, _, _ = _trunk_layer(hs, mod[1:1 + ndec], wts, shared, seq_len=n_lat, row_len=GRID_W,
                                h0f=pair(state_ssm_fwd), h0b=pair(state_ssm_bwd))
        new_f.append(hf.reshape(nb, SSM_HEADS, SSM_HEADDIM, SSM_STATE))
        new_b.append(hb.reshape(nb, SSM_HEADS, SSM_HEADDIM, SSM_STATE))
    return (hp.reshape(nb, n_ctx, d), hs.reshape(ndec, n_lat, d),
            jnp.stack(new_f, axis=1), jnp.stack(new_b, axis=1))
```

```python
import functools

import jax
import jax.numpy as jnp
from jax import lax
from jax.experimental import pallas as pl
from jax.experimental.pallas import tpu as pltpu

D_MODEL = 2048
D_FF = 5632
GRID_W = 64
CONV_W = 2048
SSM_INNER = 2048
SSM_HEADDIM = 64
SSM_HEADS = SSM_INNER // SSM_HEADDIM
SSM_GROUPS = 4
SSM_STATE = 128
CHUNK = 128
N_MOD = 9
EPS = 1e-6
XBC_W = SSM_INNER + 2 * SSM_GROUPS * SSM_STATE

IN_SPLITS = (("cb", CONV_W), ("cc", CONV_W), ("cx", CONV_W), ("z", SSM_INNER), ("xbc", XBC_W),
             ("dtf", SSM_HEADS), ("dtb", SSM_HEADS), ("gc", D_MODEL), ("gs", D_MODEL))
IN_OFFSET = {}
for _name, _width in IN_SPLITS:
    IN_OFFSET[_name] = sum(w for _, w in IN_SPLITS[:len(IN_OFFSET)])

HEADS_PER_GROUP = SSM_HEADS // SSM_GROUPS
PAIRS_PER_GROUP = HEADS_PER_GROUP // 2
GROUP_W = HEADS_PER_GROUP * SSM_HEADDIM
LANES = 128
NEG_BIG = -1e30
LOG2_E = 1.4426950408889634

TM = 512
TN = 1024
TF = 512
TM_1 = 1024
TN_1 = 1024
TN_ADA = 1024
ROW_SLAB = 16
SSD_ROWS = 1024
VMEM_LIMIT = 56 * 1024 * 1024

F32 = jnp.float32
BF16 = jnp.bfloat16


def _dot(a, b):
    return jnp.dot(a, b, preferred_element_type=F32)


def _dot_nt(a, b):
    return lax.dot_general(a, b, (((1,), (1,)), ((), ())), preferred_element_type=F32)


def _silu(x):
    return x * jax.nn.sigmoid(x)


def _rms(x, w):
    ms = jnp.mean(x * x, axis=-1, keepdims=True)
    return x * lax.rsqrt(ms + EPS) * w


def _for_row_slabs(n_rows, fn, static=False):
    if static:
        for s in range(n_rows // ROW_SLAB):
            fn(slice(s * ROW_SLAB, (s + 1) * ROW_SLAB))
        return

    def body(s, carry):
        fn(pl.ds(pl.multiple_of(s * ROW_SLAB, ROW_SLAB), ROW_SLAB))
        return carry

    lax.fori_loop(0, n_rows // ROW_SLAB, body, 0, unroll=8)


def _side_cast_specs(casts, grid):
    n_steps = grid[0] * grid[1]
    in_specs, out_specs, shapes = [], [], []
    for w, first_row, n_rows, block_rows, transposed in casts:
        n_blocks = n_rows // block_rows
        first = first_row // block_rows
        assert n_blocks * block_rows == n_rows and first * block_rows == first_row
        assert block_rows % 16 == 0 and n_blocks <= n_steps and first_row + n_rows <= w.shape[0]

        def block(i, j, n_blocks=n_blocks):
            return jnp.minimum(i * grid[1] + j, n_blocks - 1)

        in_specs.append(pl.BlockSpec((block_rows, w.shape[1]), lambda i, j, b=block, f=first: (f + b(i, j), 0)))
        if transposed:
            assert block_rows == LANES and w.shape[1] % LANES == 0
            out_specs.append(pl.BlockSpec((w.shape[1], block_rows), lambda i, j, b=block: (0, b(i, j))))
            shapes.append(jax.ShapeDtypeStruct((w.shape[1], n_rows), BF16))
        else:
            out_specs.append(pl.BlockSpec((block_rows, w.shape[1]), lambda i, j, b=block: (b(i, j), 0)))
            shapes.append(jax.ShapeDtypeStruct((n_rows, w.shape[1]), BF16))
    return in_specs, out_specs, shapes


def _convert_blocks(f32_refs, bf16_refs):
    for src, dst in zip(f32_refs, bf16_refs):
        if dst.shape == src.shape:
            dst[...] = src[...].astype(BF16)
        else:
            for k in range(src.shape[1] // LANES):
                cols = slice(k * LANES, (k + 1) * LANES)
                dst[cols, :] = src[:, cols].T.astype(BF16)


def _params(sem):
    return pltpu.CompilerParams(dimension_semantics=sem, vmem_limit_bytes=VMEM_LIMIT)


def _mod_kernel(c_ref, w_ref, b_ref, o_ref):
    a = _silu(c_ref[...]).astype(BF16)
    o_ref[...] = _dot(a, w_ref[...].astype(BF16)) + b_ref[...]


def _mod_call(cvec, w_ada, b_ada):
    n = w_ada.shape[1]
    return pl.pallas_call(
        _mod_kernel,
        grid=(n // TN_ADA,),
        in_specs=[
            pl.BlockSpec((8, D_MODEL), lambda j: (0, 0)),
            pl.BlockSpec((D_MODEL, TN_ADA), lambda j: (0, j)),
            pl.BlockSpec((1, TN_ADA), lambda j: (0, j)),
        ],
        out_specs=pl.BlockSpec((8, TN_ADA), lambda j: (0, j)),
        out_shape=jax.ShapeDtypeStruct((8, n), F32),
        compiler_params=_params(("arbitrary",)),
        name="adaln_mod",
    )(cvec, w_ada, b_ada.reshape(1, n))


def _ffn_kernel(h_ref, mod_ref, nw_ref, wg_ref, wu_ref, wd_ref, nw2_ref, *rest, mod_base, final, n_cast):
    if final:
        m_ref, wo_ref, *rest = rest
    cast_in, rest = rest[:n_cast], rest[n_cast:]
    if final:
        o_ref, *rest = rest
    else:
        o_ref, u_ref, *rest = rest
    cast_out, (u_scr, acc_scr, vec_scr) = rest[:n_cast], rest[n_cast:]
    j = pl.program_id(1)
    nj = pl.num_programs(1)

    def mod_row(k):
        return mod_ref[0, mod_base + k:mod_base + k + 1, :]

    def unit_rms(x):
        return x * lax.rsqrt(jnp.mean(x * x, axis=-1, keepdims=True) + EPS)

    @pl.when(j == 0)
    def _():
        vec_scr[0:1, :] = nw_ref[...] * (1.0 + mod_row(1))
        vec_scr[1:2, :] = 0.5 * mod_row(2)
        vec_scr[2:3, :] = nw2_ref[...] if final else nw2_ref[...] * (1.0 + mod_row(4))
        if final:
            acc_scr[...] = _dot(m_ref[...], wo_ref[...])

        def slab(r):
            x = h_ref[r, :]
            if final:
                x = x + mod_row(-1) * acc_scr[r, :]
                o_ref[r, :] = x
            u_scr[r, :] = (unit_rms(x) * vec_scr[0:1, :] + mod_row(0)).astype(BF16)
            acc_scr[r, :] = jnp.zeros((ROW_SLAB, D_MODEL), F32)

        _for_row_slabs(h_ref.shape[0], slab)

    u = u_scr[...]
    a = _silu(_dot(u, wg_ref[...])) * _dot(u, wu_ref[...])
    acc_scr[...] += _dot(a.astype(BF16), wd_ref[...])
    _convert_blocks(cast_in, cast_out)

    @pl.when(j == nj - 1)
    def _():
        def slab(r):
            if final:
                hn = o_ref[r, :] + vec_scr[1:2, :] * acc_scr[r, :]
                o_ref[r, :] = unit_rms(hn) * vec_scr[2:3, :]
            else:
                hn = h_ref[r, :] + vec_scr[1:2, :] * acc_scr[r, :]
                o_ref[r, :] = hn
                u_ref[r, :] = (unit_rms(hn) * vec_scr[2:3, :] + mod_row(3)).astype(BF16)

        _for_row_slabs(h_ref.shape[0], slab, static=True)


def _ffn_call(h, mod, nw, wg, wu, wd, nw2, merged=None, wo=None, *, mod_base, casts=()):
    t = h.shape[0]
    final = merged is not None
    grid = (t // TM, D_FF // TF)
    cast_in_specs, cast_out_specs, cast_shapes = _side_cast_specs(casts, grid)
    tpm = t // mod.shape[0] // TM
    row = pl.BlockSpec((TM, D_MODEL), lambda i, j: (i, 0))
    vec = pl.BlockSpec((1, D_MODEL), lambda i, j: (0, 0))
    in_specs = [
        row,
        pl.BlockSpec((1, N_MOD, D_MODEL), lambda i, j: (i // tpm, 0, 0)),
        vec,
        pl.BlockSpec((D_MODEL, TF), lambda i, j: (0, j)),
        pl.BlockSpec((D_MODEL, TF), lambda i, j: (0, j)),
        pl.BlockSpec((TF, D_MODEL), lambda i, j: (j, 0)),
        vec,
    ]
    args = [h, mod, nw, wg, wu, wd, nw2]
    out_shape = [jax.ShapeDtypeStruct((t, D_MODEL), F32)]
    out_specs = [row]
    if final:
        in_specs += [row, pl.BlockSpec((D_MODEL, D_MODEL), lambda i, j: (0, 0), pipeline_mode=pl.Buffered(1))]
        args += [merged, wo]
    else:
        out_shape.append(jax.ShapeDtypeStruct((t, D_MODEL), BF16))
        out_specs.append(row)
    return pl.pallas_call(
        functools.partial(_ffn_kernel, mod_base=mod_base, final=final, n_cast=len(casts)),
        grid=grid,
        in_specs=in_specs + cast_in_specs,
        out_specs=out_specs + cast_out_specs,
        out_shape=out_shape + cast_shapes,
        scratch_shapes=[pltpu.VMEM((TM, D_MODEL), BF16), pltpu.VMEM((TM, D_MODEL), F32),
                        pltpu.VMEM((8, D_MODEL), F32)],
        compiler_params=_params(("arbitrary" if casts else "parallel", "arbitrary")),
        name="ffn_final" if final else "ffn",
    )(*args, *[entry[0] for entry in casts])


def _softplus(x):
    return jnp.maximum(x, 0.0) + jnp.log1p(jnp.exp(-jnp.abs(x)))


def _convproj_kernel(u_ref, wb_ref, wc_ref, wx_ref, cw_ref, wdt_ref, dtb_ref, *rest, row_len, n_cast):
    cast_in, (o_ref, dt_ref), cast_out = rest[:n_cast], rest[n_cast:n_cast + 2], rest[n_cast + 2:]
    u = u_ref[...]

    @pl.when(pl.program_id(1) == 0)
    def _():
        dt_ref[...] = _softplus(_dot_nt(u, wdt_ref[...]) + dtb_ref[...])

    v = _dot(u, wc_ref[...]) * _dot(u, wx_ref[...])
    pos = lax.broadcasted_iota(jnp.int32, v.shape, 0) & (row_len - 1)
    v_prev = jnp.where(pos == 0, 0.0, pltpu.roll(v, 1, 0))
    v_next = jnp.where(pos == row_len - 1, 0.0, pltpu.roll(v, TM - 1, 0))
    cw = cw_ref[...]
    conv = cw[0:1, :] * v_prev + cw[1:2, :] * v + cw[2:3, :] * v_next
    o_ref[...] = (_dot(u, wb_ref[...]) * conv).astype(BF16)
    _convert_blocks(cast_in, cast_out)


def _w_in_cols(name, tn):
    first = IN_OFFSET[name] // tn
    assert first * tn == IN_OFFSET[name]
    return pl.BlockSpec((D_MODEL, tn), lambda i, j: (0, first + j))


def _convproj_call(u, w_in_bf, conv_w, w_dt_t, dt_bias, *, row_len, casts=()):
    t = u.shape[0]
    ndt = w_dt_t.shape[0]
    grid = (t // TM, CONV_W // TN)
    cast_in_specs, cast_out_specs, cast_shapes = _side_cast_specs(casts, grid)
    return pl.pallas_call(
        functools.partial(_convproj_kernel, row_len=row_len, n_cast=len(casts)),
        grid=grid,
        in_specs=[
            pl.BlockSpec((TM, D_MODEL), lambda i, j: (i, 0)),
            _w_in_cols("cb", TN), _w_in_cols("cc", TN), _w_in_cols("cx", TN),
            pl.BlockSpec((3, TN), lambda i, j: (0, j)),
            pl.BlockSpec((ndt, D_MODEL), lambda i, j: (0, 0)),
            pl.BlockSpec((1, ndt), lambda i, j: (0, 0)),
        ] + cast_in_specs,
        out_specs=[
            pl.BlockSpec((TM, TN), lambda i, j: (i, j)),
            pl.BlockSpec((TM, ndt), lambda i, j: (i, 0)),
        ] + cast_out_specs,
        out_shape=[
            jax.ShapeDtypeStruct((t, CONV_W), BF16),
            jax.ShapeDtypeStruct((t, ndt), F32),
        ] + cast_shapes,
        compiler_params=_params(("parallel", "arbitrary")),
        name="conv_proj",
    )(u, w_in_bf, w_in_bf, w_in_bf, conv_w, w_dt_t, dt_bias, *[entry[0] for entry in casts])


def _xbcproj_kernel(u_ref, w_ref, cw_ref, cb_ref, o_ref, *, seq_len):
    v = _dot(u_ref[...], w_ref[...])
    rows = v.shape[0]
    pos = lax.broadcasted_iota(jnp.int32, v.shape, 0) & (seq_len - 1)
    v_prev = jnp.where(pos == 0, 0.0, pltpu.roll(v, 1, 0))
    v_next = jnp.where(pos == seq_len - 1, 0.0, pltpu.roll(v, rows - 1, 0))
    cw = cw_ref[...]
    o_ref[...] = _silu(cw[0:1, :] * v_prev + cw[1:2, :] * v + cw[2:3, :] * v_next + cb_ref[...])


def _xbcproj_call(u, w_in_bf, conv_w, conv_b, *, seq_len):
    t = u.shape[0]
    tm = max(TM_1, seq_len)
    tn = TN_1 * TM_1 // tm
    assert tm % seq_len == 0 and seq_len & (seq_len - 1) == 0
    return pl.pallas_call(
        functools.partial(_xbcproj_kernel, seq_len=seq_len),
        grid=(t // tm, XBC_W // tn),
        in_specs=[
            pl.BlockSpec((tm, D_MODEL), lambda i, j: (i, 0)),
            _w_in_cols("xbc", tn),
            pl.BlockSpec((3, tn), lambda i, j: (0, j)),
            pl.BlockSpec((1, tn), lambda i, j: (0, j)),
        ],
        out_specs=pl.BlockSpec((tm, tn), lambda i, j: (i, j)),
        out_shape=jax.ShapeDtypeStruct((t, XBC_W), F32),
        compiler_params=_params(("parallel", "arbitrary")),
        name="xbc_proj",
    )(u, w_in_bf, conv_w, conv_b)


def _split3(a):
    a1 = a.astype(BF16)
    r1 = a - a1.astype(F32)
    a2 = r1.astype(BF16)
    a3 = (r1 - a2.astype(F32)).astype(BF16)
    return a1, a2, a3


def _ssd_kernel(x_ref, b_ref, c_ref, dt_ref, alog_ref, dsk_ref, *rest, n_seq, seq_len, has_h0, n_cast):
    cast_in, rest = rest[:n_cast], rest[n_cast:]
    if has_h0:
        h0f_ref, h0b_ref, y_ref, *rest = rest
    else:
        y_ref, hf_ref, hb_ref, *rest = rest
    cast_out, scratch = rest[:n_cast], rest[n_cast:]
    x_st, b_bf, c_bf, cum_t, dt_t, w_t, dec, st_f, st_b = scratch
    q = CHUNK
    cps = seq_len // q
    nc = n_seq * cps
    cast_rows = [src.shape[0] // nc for src in cast_in]
    assert all(n * nc == src.shape[0] and n % 16 == 0 for n, src in zip(cast_rows, cast_in))
    hpg = HEADS_PER_GROUP
    nh = 2 * hpg

    lane = lax.broadcasted_iota(jnp.int32, (q, LANES), 1)
    row = lax.broadcasted_iota(jnp.int32, (q, q), 0)
    col = lax.broadcasted_iota(jnp.int32, (q, q), 1)
    tril = row >= col
    triu = row <= col
    lo_half = lane < SSM_HEADDIM
    lo_half_row = lo_half[0:1, :]

    def rows(c):
        return pl.ds(pl.multiple_of(c * q, q), q)

    def pair(p):
        return slice(p * LANES, (p + 1) * LANES)

    def cumsums(c, carry):
        r = rows(c)
        dt = dt_ref[r, :]
        a_row = jnp.where(lane[0:1, :] < nh, -LOG2_E * jnp.exp(alog_ref[...]), 0.0)
        a = dt * a_row
        tril_b = jnp.where(tril, 1.0, 0.0).astype(BF16)
        p3 = _dot(tril_b, jnp.concatenate(_split3(a), axis=1))
        pre = p3[:, 0:LANES] + p3[:, LANES:2 * LANES] + p3[:, 2 * LANES:]
        suf = (pre[q - 1:q, :] - pre) + a
        cm_t = jnp.where(lane < hpg, pre, suf).T[0:nh, :]
        dtt = dt.T[0:nh, :]
        cum_t[c] = cm_t
        dt_t[c] = dtt

        is_fwd = lax.broadcasted_iota(jnp.int32, (nh, q), 0) < hpg
        edge = jnp.where(is_fwd, cm_t[:, q - 1:q], cm_t[:, 0:1])
        w_t[c] = jnp.exp2(edge - cm_t) * dtt
        dec[c] = jnp.exp2(edge)
        return carry

    lax.fori_loop(0, nc, cumsums, 0, unroll=4)

    def own_states(c, carry):
        r = rows(c)
        x = x_ref[r, :]
        bm = b_ref[r, :]
        y_ref[r, :] = dsk_ref[...] * x
        lo_x = (lax.broadcasted_iota(jnp.int32, x.shape, 1) & (LANES - 1)) < SSM_HEADDIM
        x2 = jnp.concatenate([jnp.where(lo_x, x, 0.0), jnp.where(lo_x, 0.0, x)], axis=0).astype(BF16)
        x_st[c] = x2
        for n, src, dst in zip(cast_rows, cast_in, cast_out):
            part = pl.ds(pl.multiple_of(c * n, 16), n)
            dst[part, :] = src[part, :].astype(BF16)
        b_bf[r, :] = bm.astype(BF16)
        c_bf[r, :] = c_ref[r, :].astype(BF16)
        w = w_t[c]
        b_t = bm.T
        for p in range(PAIRS_PER_GROUP):
            lhs = jnp.concatenate(
                [jnp.concatenate([b_t * w[h:h + 1, :], b_t * w[h + 1:h + 2, :]], axis=1)
                 for h in (2 * p, hpg + 2 * p)], axis=0).astype(BF16)
            own = _dot(lhs, x2[:, pair(p)])
            st_f[c, :, pair(p)] = own[0:SSM_STATE, :]
            st_b[c, :, pair(p)] = own[SSM_STATE:, :]
        return carry

    lax.fori_loop(0, nc, own_states, 0, unroll=2)

    def pair_decay(c, h):
        d = dec[c]
        return jnp.where(lo_half_row, d[h:h + 1, :], d[h + 1:h + 2, :])

    for p in range(PAIRS_PER_GROUP):
        for s in range(n_seq):
            first = s * cps
            if has_h0:
                init = (h0f_ref[s, p].T, h0b_ref[s, p].T)
            else:
                init = (jnp.zeros((SSM_STATE, LANES), F32), jnp.zeros((SSM_STATE, LANES), F32))

            def chain(k, carry, p=p, first=first):
                sf, sb = carry
                cf = first + k
                cb = first + cps - 1 - k
                own_f = st_f[cf, :, pair(p)]
                own_b = st_b[cb, :, pair(p)]
                st_f[cf, :, pair(p)] = sf
                st_b[cb, :, pair(p)] = sb
                return (pair_decay(cf, 2 * p) * sf + own_f, pair_decay(cb, hpg + 2 * p) * sb + own_b)

            sf, sb = lax.fori_loop(0, cps, chain, init)
            if not has_h0:
                hf_ref[s, p] = sf.T
                hb_ref[s, p] = sb.T

    def main(c, carry):
        r = rows(c)
        cm_t = cum_t[c]
        dtt = dt_t[c]
        cb = c_bf[r, :]
        g = _dot_nt(cb, b_bf[r, :])
        off_f = _dot(cb, st_f[c].astype(BF16))
        off_b = _dot(cb, st_b[c].astype(BF16))
        for p in range(PAIRS_PER_GROUP):
            ms, col_f, col_b = [], [], []
            for e in range(2):
                hf = 2 * p + e
                hb = hpg + hf
                ci_f = jnp.broadcast_to(cm_t[hf:hf + 1, :], (q, q)).T
                ci_b = jnp.broadcast_to(cm_t[hb:hb + 1, :], (q, q)).T
                m_f = jnp.exp2(jnp.where(tril, ci_f - cm_t[hf:hf + 1, :], NEG_BIG)) * dtt[hf:hf + 1, :]
                m_b = jnp.exp2(jnp.where(triu, ci_b - cm_t[hb:hb + 1, :], NEG_BIG)) * dtt[hb:hb + 1, :]
                ms.append(((m_f + m_b) * g).astype(BF16))
                col_f.append(ci_f)
                col_b.append(ci_b)
            e_f = jnp.exp2(jnp.where(lo_half, col_f[0], col_f[1]))
            e_b = jnp.exp2(jnp.where(lo_half, col_b[0], col_b[1]))
            y_ref[r, pair(p)] += (_dot(jnp.concatenate(ms, axis=1), x_st[c, :, pair(p)])
                                  + e_f * off_f[:, pair(p)] + e_b * off_b[:, pair(p)])
        return carry

    lax.fori_loop(0, nc, main, 0, unroll=2)


def _ssd_call(xbc, dt, alog, dskip, h0f, h0b, *, seq_len, casts=()):
    t = xbc.shape[0]
    n_seq = max(1, SSD_ROWS // seq_len)
    blk = n_seq * seq_len
    nc = blk // CHUNK
    has_h0 = h0f is not None
    gx = SSM_INNER // LANES
    gc = gx + SSM_GROUPS
    st_spec = pl.BlockSpec((n_seq, PAIRS_PER_GROUP, LANES, SSM_STATE), lambda s, g: (s, g, 0, 0))
    in_specs = [
        pl.BlockSpec((blk, GROUP_W), lambda s, g: (s, g)),
        pl.BlockSpec((blk, LANES), lambda s, g: (s, gx + g)),
        pl.BlockSpec((blk, LANES), lambda s, g: (s, gc + g)),
        pl.BlockSpec((blk, LANES), lambda s, g: (s, g)),
        pl.BlockSpec((1, LANES), lambda s, g: (0, g)),
        pl.BlockSpec((1, GROUP_W), lambda s, g: (0, g)),
    ]
    args = [xbc, xbc, xbc, dt, alog, dskip]
    grid = (t // blk, SSM_GROUPS)
    cast_in_specs, cast_out_specs, cast_shapes = _side_cast_specs(casts, grid)
    in_specs += cast_in_specs
    args += [entry[0] for entry in casts]
    y_shape = jax.ShapeDtypeStruct((t, SSM_INNER), F32)
    y_spec = pl.BlockSpec((blk, GROUP_W), lambda s, g: (s, g))
    st_shape = jax.ShapeDtypeStruct((t // seq_len, SSM_HEADS // 2, LANES, SSM_STATE), F32)
    if has_h0:
        in_specs += [st_spec, st_spec]
        args += [h0f, h0b]
        out_specs, out_shape = [y_spec], [y_shape]
    else:
        out_specs, out_shape = [y_spec, st_spec, st_spec], [y_shape, st_shape, st_shape]
    out_specs += cast_out_specs
    out_shape += cast_shapes
    chunk_states = (nc, SSM_STATE, GROUP_W)
    return pl.pallas_call(
        functools.partial(_ssd_kernel, n_seq=n_seq, seq_len=seq_len, has_h0=has_h0, n_cast=len(casts)),
        grid=grid,
        in_specs=in_specs,
        out_specs=out_specs,
        out_shape=out_shape,
        scratch_shapes=[
            pltpu.VMEM((nc, 2 * CHUNK, GROUP_W), BF16),
            pltpu.VMEM((blk, LANES), BF16),
            pltpu.VMEM((blk, LANES), BF16),
            pltpu.VMEM((nc, 2 * HEADS_PER_GROUP, CHUNK), F32),
            pltpu.VMEM((nc, 2 * HEADS_PER_GROUP, CHUNK), F32),
            pltpu.VMEM((nc, 2 * HEADS_PER_GROUP, CHUNK), F32),
            pltpu.VMEM((nc, 2 * HEADS_PER_GROUP, CHUNK), F32),
            pltpu.VMEM(chunk_states, F32),
            pltpu.VMEM(chunk_states, F32),
        ],
        compiler_params=_params(("parallel", "arbitrary")),
        name="ssd_latent" if has_h0 else "ssd_context",
    )(*args)


def _gatednorm_kernel(u_ref, y_ref, wz_ref, nw_ref, o_ref, t_scr, ss_scr):
    j = pl.program_id(1)
    nj = SSM_INNER // TN_1
    t = y_ref[...] * _silu(_dot(u_ref[...], wz_ref[...]))
    t_scr[j] = t
    part = jnp.sum(t * t, axis=-1, keepdims=True)

    @pl.when(j == 0)
    def _():
        ss_scr[...] = part

    @pl.when(j > 0)
    def _():
        ss_scr[...] += part

    @pl.when(j == nj - 1)
    def _():
        def slab(r):
            inv = lax.rsqrt(ss_scr[r, :] * (1.0 / SSM_INNER) + EPS)
            for k in range(nj):
                sl = slice(k * TN_1, (k + 1) * TN_1)
                o_ref[r, sl] = (t_scr[k, r, :] * inv * nw_ref[:, sl]).astype(BF16)

        _for_row_slabs(o_ref.shape[0], slab)


def _gatednorm_call(u, y, w_in_bf, nw):
    t = u.shape[0]
    nj = SSM_INNER // TN_1
    return pl.pallas_call(
        _gatednorm_kernel,
        grid=(t // TM_1, nj),
        in_specs=[
            pl.BlockSpec((TM_1, D_MODEL), lambda i, j: (i, 0)),
            pl.BlockSpec((TM_1, TN_1), lambda i, j: (i, j)),
            _w_in_cols("z", TN_1),
            pl.BlockSpec((1, SSM_INNER), lambda i, j: (0, 0)),
        ],
        out_specs=pl.BlockSpec((TM_1, SSM_INNER), lambda i, j: (i, 0)),
        out_shape=jax.ShapeDtypeStruct((t, SSM_INNER), BF16),
        scratch_shapes=[pltpu.VMEM((nj, TM_1, TN_1), F32), pltpu.VMEM((TM_1, 1), F32)],
        compiler_params=_params(("parallel", "arbitrary")),
        name="gated_norm",
    )(u, y, w_in_bf, nw)


def _merge_kernel(u_ref, cv_ref, yn_ref, wgc_ref, wco_ref, wgs_ref, wso_ref, o_ref):
    u = u_ref[...]
    m = jax.nn.sigmoid(_dot_nt(u, wgc_ref[...])) * _dot(cv_ref[...], wco_ref[...])
    m += jax.nn.sigmoid(_dot_nt(u, wgs_ref[...])) * _dot(yn_ref[...], wso_ref[...])
    o_ref[...] = m.astype(BF16)


def _merge_call(u, cv, yn, w_gate_t, wco, wso):
    t = u.shape[0]
    row = pl.BlockSpec((TM, D_MODEL), lambda i, j: (i, 0))
    wspec = pl.BlockSpec((D_MODEL, TN), lambda i, j: (0, j))
    wspec_gc = pl.BlockSpec((TN, D_MODEL), lambda i, j: (j, 0))
    wspec_gs = pl.BlockSpec((TN, D_MODEL), lambda i, j: (D_MODEL // TN + j, 0))
    return pl.pallas_call(
        _merge_kernel,
        grid=(t // TM, D_MODEL // TN),
        in_specs=[row, row, row, wspec_gc, wspec, wspec_gs, wspec],
        out_specs=pl.BlockSpec((TM, TN), lambda i, j: (i, j)),
        out_shape=jax.ShapeDtypeStruct((t, D_MODEL), BF16),
        compiler_params=_params(("parallel", "arbitrary")),
        name="merge",
    )(u, cv, yn, w_gate_t, wco, w_gate_t, wso)


def _group_lanes(f, b):
    lead = f.shape[:-1]
    fb = jnp.concatenate([f.reshape(lead + (SSM_GROUPS, HEADS_PER_GROUP)),
                          b.reshape(lead + (SSM_GROUPS, HEADS_PER_GROUP))], axis=-1)
    pad = [(0, 0)] * (fb.ndim - 1) + [(0, LANES - 2 * HEADS_PER_GROUP)]
    return jnp.pad(fb, pad).reshape(lead + (SSM_GROUPS * LANES,))


SIDE_CASTS = {
    "ffn": {"w_in": ("w_in_t", 0, IN_OFFSET["dtf"], 128, True),
            "w_dt_rows": ("w_in_t", IN_OFFSET["dtf"], 2 * SSM_HEADS, 64, False),
            "w_gate_t": ("w_in_t", IN_OFFSET["gc"], 2 * D_MODEL, 64, False)},
    "conv_proj": {k: (k, 0, D_MODEL, 128, False) for k in ("w_co", "w_so", "w_o")},
    "ssd": {"f2g": ("f2g", 0, D_MODEL, 128, False), "f2u": ("f2u", 0, D_MODEL, 128, False),
            "f2d": ("f2d", 0, D_MODEL, 128, False)},
}


def _trunk_layer(h, mod, wts, shared, *, seq_len, row_len, h0f, h0b):
    produce = shared is None
    shared = {} if produce else shared

    def casts(host):
        return tuple((wts[src], *where) for src, *where in SIDE_CASTS[host].values()) if produce else ()

    def split(host, outs):
        n = len(SIDE_CASTS[host]) if produce else 0
        shared.update(zip(SIDE_CASTS[host], outs[len(outs) - n:]))
        return outs[:len(outs) - n]

    h1, u2 = split("ffn", _ffn_call(h, mod, wts["norm1"], wts["f1g"], wts["f1u"], wts["f1d"], wts["norm2"],
                                    mod_base=0, casts=casts("ffn")))
    w_in_bf = shared["w_in"]
    dt_rows = shared["w_dt_rows"]
    w_dt_t = _group_lanes(dt_rows[:SSM_HEADS].T, dt_rows[SSM_HEADS:].T).T
    cv, dt = split("conv_proj", _convproj_call(u2, w_in_bf, wts["conv_w"], w_dt_t, wts["dt_bias"],
                                               row_len=row_len, casts=casts("conv_proj")))
    xbc = _xbcproj_call(u2, w_in_bf, wts["ssm_conv_w"], wts["ssm_conv_b"], seq_len=seq_len)
    ssd = split("ssd", _ssd_call(xbc, dt, wts["a_log"], wts["d_skip"], h0f, h0b, seq_len=seq_len, casts=casts("ssd")))
    yn = _gatednorm_call(u2, ssd[0], w_in_bf, wts["ssm_norm"])
    merged = _merge_call(u2, cv, yn, shared["w_gate_t"], shared["w_co"], shared["w_so"])
    out = _ffn_call(h1, mod, wts["norm3"], shared["f2g"], shared["f2u"], shared["f2d"].reshape(D_FF, D_MODEL),
                    wts["final_norm"],
                    merged, shared["w_o"], mod_base=6)[0]
    return out, ssd[1:], shared


def kernel(x_prompt, x_sample, c, state_ssm_fwd, state_ssm_bwd, c_ctx, w_ada, b_ada, norm1_w, ffn1_w_gate, ffn1_w_up, ffn1_w_down, norm2_w, w_in, conv_w, ssm_conv_w, ssm_conv_b, dt_bias_f, dt_bias_b, a_log_f, a_log_b, d_skip, ssm_norm_w, w_conv_out, w_ssm_out, w_o, norm3_w, ffn2_w_gate, ffn2_w_up, ffn2_w_down, final_norm_w):
    depth = w_ada.shape[0]
    assert depth == 1, "the second ffn call is fused with the final norm, so exactly one layer is supported"
    nb, n_ctx, d = x_prompt.shape
    ndec, n_lat, _ = x_sample.shape
    hp = x_prompt.reshape(nb * n_ctx, d)
    hs = x_sample.reshape(ndec * n_lat, d)
    cvec = jnp.zeros((8, d), F32).at[0].set(c_ctx).at[1:1 + ndec].set(c)
    vec = lambda v: v.reshape(1, -1)
    bf = lambda w: w.astype(BF16)
    new_f, new_b = [], []
    for l in range(depth):
        mod = _mod_call(cvec, w_ada[l], b_ada[l]).reshape(8, N_MOD, d)
        w_in_t = w_in[l].T
        wts = dict(
            w_in_t=w_in_t,
            dt_bias=vec(_group_lanes(dt_bias_f[l], dt_bias_b[l])),
            a_log=vec(_group_lanes(a_log_f[l], a_log_b[l])),
            d_skip=vec(jnp.repeat(d_skip[l], SSM_HEADDIM)),
            norm1=vec(norm1_w[l]), norm2=vec(norm2_w[l]), norm3=vec(norm3_w[l]), ssm_norm=vec(ssm_norm_w[l]),
            final_norm=vec(final_norm_w),
            f1g=bf(ffn1_w_gate[l]), f1u=bf(ffn1_w_up[l]), f1d=bf(ffn1_w_down[l]),
            conv_w=conv_w[l], ssm_conv_w=ssm_conv_w[l], ssm_conv_b=vec(ssm_conv_b[l]),
            f2g=ffn2_w_gate[l], f2u=ffn2_w_up[l], f2d=ffn2_w_down[l].reshape(d, D_FF),
            w_co=w_conv_out[l], w_so=w_ssm_out[l], w_o=w_o[l],
        )
        pair = lambda s: s[:, l].reshape(ndec, SSM_HEADS // 2, 2 * SSM_HEADDIM, SSM_STATE)

        hp, (hf, hb), shared = _trunk_layer(hp, mod[0:1], wts, None, seq_len=n_ctx, row_len=n_ctx, h0f=None, h0b=None)
        hs, _, _ = _trunk_layer(hs, mod[1:1 + ndec], wts, shared, seq_len=n_lat, row_len=GRID_W,
                                h0f=pair(state_ssm_fwd), h0b=pair(state_ssm_bwd))
        new_f.append(hf.reshape(nb, SSM_HEADS, SSM_HEADDIM, SSM_STATE))
        new_b.append(hb.reshape(nb, SSM_HEADS, SSM_HEADDIM, SSM_STATE))
    return (hp.reshape(nb, n_ctx, d), hs.reshape(ndec, n_lat, d),
            jnp.stack(new_f, axis=1), jnp.stack(new_b, axis=1))
```
